```python
import jax, jax.numpy as jnp
from jax import lax
import numpy as np

D_MODEL = 2048
BATCH = 4
SEQ = 2048
DEPTH = 1
DEC_BATCH = 128
DEC_SEQ = 1
PAST_LEN = 8192
PAGE_SIZE = 128

HEAD_DIM = 64
MIX_WIDTH = D_MODEL
RWKV_WIDTH = MIX_WIDTH // 2
ATTN_WIDTH = MIX_WIDTH - RWKV_WIDTH
N_RWKV_HEADS = RWKV_WIDTH // HEAD_DIM
N_Q_HEADS = ATTN_WIDTH // HEAD_DIM
N_KV_HEADS = N_Q_HEADS // 4
GQA_GROUP = N_Q_HEADS // N_KV_HEADS
KV_WIDTH = N_KV_HEADS * HEAD_DIM
WINDOW = 128
ATTN_BLOCK = 128
DECAY_LORA = 64
ICLR_LORA = 64
GATE_LORA = 160
D_FF = 5504
RMS_EPS = 1e-6
GN_EPS = 64e-5
NEG_INF = -1e30
ATTN_PROJ = ATTN_WIDTH + 2 * KV_WIDTH
SHIFT_WIDTH = 3 * RWKV_WIDTH + DECAY_LORA + ICLR_LORA + GATE_LORA
IN_PROJ = ATTN_PROJ + SHIFT_WIDTH

kernel_name = 'hymba_rwkv7_swa_sink_alibi_macaron_step'


def _rmsnorm(x, g):
    xf = x.astype(jnp.float32)
    y = xf * lax.rsqrt(jnp.mean(xf * xf, axis=-1, keepdims=True) + RMS_EPS)
    return (y * g.astype(jnp.float32)).astype(x.dtype)


def _swiglu(x, w_gate, w_up, w_down):
    return (jax.nn.silu(x @ w_gate) * (x @ w_up)) @ w_down


def _alibi_slopes():
    return jnp.exp2(-8.0 * jnp.arange(1, N_Q_HEADS + 1, dtype=jnp.float32) / N_Q_HEADS)


def _window_attention(q, k, v, valid, dist, sinks):
    s = jnp.einsum('...qcgd,...kcd->...cgqk', q, k).astype(jnp.float32) * (HEAD_DIM ** -0.5)
    slopes = _alibi_slopes().reshape(N_KV_HEADS, GQA_GROUP, 1, 1)
    s = jnp.where(valid, s - slopes * dist.astype(jnp.float32), NEG_INF)
    sink = jnp.broadcast_to(sinks.astype(jnp.float32).reshape(N_KV_HEADS, GQA_GROUP, 1, 1), s.shape[:-1] + (1,))
    p = jax.nn.softmax(jnp.concatenate([s, sink], axis=-1), axis=-1)[..., :-1]
    return jnp.einsum('...cgqk,...kcd->...qcgd', p.astype(v.dtype), v)


def _swa_prompt(q, k, v, sinks):
    B, T = q.shape[:2]
    nb = T // ATTN_BLOCK
    qb = q.reshape(B, nb, ATTN_BLOCK, N_KV_HEADS, GQA_GROUP, HEAD_DIM)

    def banded(t):
        tb = t.reshape(B, nb, ATTN_BLOCK, N_KV_HEADS, HEAD_DIM)
        prev = jnp.pad(tb, ((0, 0), (1, 0), (0, 0), (0, 0), (0, 0)))[:, :-1]
        return jnp.concatenate([prev, tb], axis=2)

    kb, vb = banded(k), banded(v)
    qi = jnp.arange(ATTN_BLOCK)[:, None] + ATTN_BLOCK
    kj = jnp.arange(2 * ATTN_BLOCK)[None, :]
    dist = qi - kj
    in_window = (dist >= 0) & (dist < WINDOW)
    has_prev = (jnp.arange(nb) > 0)[:, None, None] | (kj >= ATTN_BLOCK)[None]
    valid = (in_window[None] & has_prev)[:, None, None]
    o = _window_attention(qb, kb, vb, valid, dist, sinks)
    wb = min(WINDOW, T)
    return o.reshape(B, T, ATTN_WIDTH), k[:, T - wb:], v[:, T - wb:]


def _swa_sample(q, k, v, cache_k, cache_v, sinks):
    B, S = q.shape[:2]
    wb = cache_k.shape[1]
    keys = jnp.concatenate([cache_k.astype(k.dtype), k], axis=1)
    vals = jnp.concatenate([cache_v.astype(v.dtype), v], axis=1)
    qpos = PAST_LEN + jnp.arange(S)
    kpos = PAST_LEN - wb + jnp.arange(wb + S)
    dist = qpos[:, None] - kpos[None, :]
    valid = (dist >= 0) & (dist < WINDOW)
    o = _window_attention(q, keys, vals, valid, dist, sinks)
    return o.reshape(B, S, ATTN_WIDTH), keys[:, S:], vals[:, S:]


def _rwkv7_mix(p, shift_prev, wkv0, W):
    B, T, _ = p.shape
    f32 = jnp.float32
    prev = jnp.concatenate([shift_prev[:, None].astype(p.dtype), p[:, :-1]], axis=1)
    u = p + W['rwkv_mu'] * (prev - p)
    r, k, v, dd, da, dg = jnp.split(u, [RWKV_WIDTH, 2 * RWKV_WIDTH, 3 * RWKV_WIDTH,
                                       3 * RWKV_WIDTH + DECAY_LORA,
                                       3 * RWKV_WIDTH + DECAY_LORA + ICLR_LORA], axis=-1)
    w_log = -jax.nn.softplus(-(W['rwkv_decay_base'] + jnp.tanh(dd) @ W['rwkv_decay_up']).astype(f32)) - 0.5
    decay = jnp.exp(-jnp.exp(w_log))
    a = jax.nn.sigmoid((W['rwkv_iclr_base'] + da @ W['rwkv_iclr_up']).astype(f32))
    g = jax.nn.sigmoid(dg) @ W['rwkv_gate_up']

    def heads(t):
        return t.reshape(B, T, N_RWKV_HEADS, HEAD_DIM)

    r32, k32, v32 = heads(r.astype(f32)), heads(k.astype(f32)), heads(v.astype(f32))
    a, decay = heads(a), heads(decay)
    kk = k32 * W['rwkv_k_k'].astype(f32).reshape(N_RWKV_HEADS, HEAD_DIM)
    kk = kk / jnp.maximum(jnp.sqrt(jnp.sum(kk * kk, axis=-1, keepdims=True)), 1e-12)
    k32 = k32 * (1.0 + (a - 1.0) * W['rwkv_k_a'].astype(f32).reshape(N_RWKV_HEADS, HEAD_DIM))

    def step(S, xs):
        r_t, w_t, k_t, v_t, na_t, b_t = xs
        sa = jnp.einsum('bhvk,bhk->bhv', S, na_t)
        S = S * w_t[:, :, None, :] + sa[..., None] * b_t[:, :, None, :] + v_t[..., None] * k_t[:, :, None, :]
        return S, jnp.einsum('bhvk,bhk->bhv', S, r_t)

    xs = tuple(jnp.moveaxis(t, 1, 0) for t in (r32, decay, k32, v32, -kk, kk * a))
    S_T, o = lax.scan(step, wkv0.astype(f32), xs)
    o = jnp.moveaxis(o, 0, 1)
    mean = jnp.mean(o, axis=-1, keepdims=True)
    var = jnp.mean(jnp.square(o - mean), axis=-1, keepdims=True)
    o = (o - mean) * lax.rsqrt(var + GN_EPS)
    o = o * W['rwkv_lnx_g'].astype(f32).reshape(N_RWKV_HEADS, HEAD_DIM) + W['rwkv_lnx_b'].astype(f32).reshape(N_RWKV_HEADS, HEAD_DIM)
    bonus = jnp.sum(r32 * k32 * W['rwkv_r_k'].astype(f32), axis=-1, keepdims=True) * v32
    o = (o + bonus).reshape(B, T, RWKV_WIDTH).astype(p.dtype) * g
    return o, p[:, -1], S_T.astype(wkv0.dtype)


def _layer(x, shift_prev, wkv0, cache_k, cache_v, W):
    B, T, _ = x.shape
    x = x + 0.5 * _swiglu(_rmsnorm(x, W['ffn1_norm']), W['ffn1_w_gate'], W['ffn1_w_up'], W['ffn1_w_down'])
    p = _rmsnorm(x, W['mix_norm']) @ W['w_in']
    q = p[..., :ATTN_WIDTH].reshape(B, T, N_KV_HEADS, GQA_GROUP, HEAD_DIM)
    k = p[..., ATTN_WIDTH:ATTN_WIDTH + KV_WIDTH].reshape(B, T, N_KV_HEADS, HEAD_DIM)
    v = p[..., ATTN_WIDTH + KV_WIDTH:ATTN_PROJ].reshape(B, T, N_KV_HEADS, HEAD_DIM)
    if cache_k is None:
        o_attn, k_win, v_win = _swa_prompt(q, k, v, W['attn_sinks'])
    else:
        o_attn, k_win, v_win = _swa_sample(q, k, v, cache_k, cache_v, W['attn_sinks'])
    o_rwkv, shift_new, wkv_new = _rwkv7_mix(p[..., ATTN_PROJ:], shift_prev, wkv0, W)
    x = x + jnp.concatenate([o_rwkv, o_attn], axis=-1) @ W['w_out']
    x = x + 0.5 * _swiglu(_rmsnorm(x, W['ffn2_norm']), W['ffn2_w_gate'], W['ffn2_w_up'], W['ffn2_w_down'])
    return x, shift_new, wkv_new, k_win, v_win


def setup_inputs(seed: int = 0) -> dict:
    key = jax.random.key(seed)
    ks = iter(jax.random.split(key, 40))
    f32 = jnp.float32
    L = DEPTH
    w_buf = min(WINDOW, PAST_LEN)

    def nrm(shape, scale):
        return scale * jax.random.normal(next(ks), shape, f32)

    return {
        'x_prompt': nrm((BATCH, SEQ, D_MODEL), 1.0),
        'x_sample': nrm((DEC_BATCH, DEC_SEQ, D_MODEL), 1.0),
        'state_rwkv_shift': nrm((L, DEC_BATCH, SHIFT_WIDTH), 1.0),
        'state_rwkv_wkv': nrm((L, DEC_BATCH, N_RWKV_HEADS, HEAD_DIM, HEAD_DIM), 0.5),
        'cache_swa_k': nrm((L, DEC_BATCH, w_buf, N_KV_HEADS, HEAD_DIM), 1.0),
        'cache_swa_v': nrm((L, DEC_BATCH, w_buf, N_KV_HEADS, HEAD_DIM), 1.0),
        'ffn1_norm': 1.0 + nrm((L, D_MODEL), 0.01),
        'ffn1_w_gate': nrm((L, D_MODEL, D_FF), D_MODEL ** -0.5),
        'ffn1_w_up': nrm((L, D_MODEL, D_FF), D_MODEL ** -0.5),
        'ffn1_w_down': nrm((L, D_FF, D_MODEL), D_FF ** -0.5),
        'mix_norm': 1.0 + nrm((L, D_MODEL), 0.01),
        'w_in': nrm((L, D_MODEL, IN_PROJ), D_MODEL ** -0.5),
        'rwkv_mu': jax.random.uniform(next(ks), (L, SHIFT_WIDTH), f32),
        'rwkv_decay_up': nrm((L, DECAY_LORA, RWKV_WIDTH), 0.1),
        'rwkv_decay_base': -1.6 + nrm((L, RWKV_WIDTH), 0.5),
        'rwkv_iclr_up': nrm((L, ICLR_LORA, RWKV_WIDTH), 0.1),
        'rwkv_iclr_base': nrm((L, RWKV_WIDTH), 0.1),
        'rwkv_gate_up': nrm((L, GATE_LORA, RWKV_WIDTH), GATE_LORA ** -0.5),
        'rwkv_k_k': 0.85 + nrm((L, RWKV_WIDTH), 0.02),
        'rwkv_k_a': 1.0 + nrm((L, RWKV_WIDTH), 0.02),
        'rwkv_r_k': nrm((L, N_RWKV_HEADS, HEAD_DIM), 0.1),
        'rwkv_lnx_g': 1.0 + nrm((L, RWKV_WIDTH), 0.01),
        'rwkv_lnx_b': nrm((L, RWKV_WIDTH), 0.01),
        'attn_sinks': nrm((L, N_Q_HEADS), 0.5),
        'w_out': nrm((L, MIX_WIDTH, D_MODEL), MIX_WIDTH ** -0.5),
        'ffn2_norm': 1.0 + nrm((L, D_MODEL), 0.01),
        'ffn2_w_gate': nrm((L, D_MODEL, D_FF), D_MODEL ** -0.5),
        'ffn2_w_up': nrm((L, D_MODEL, D_FF), D_MODEL ** -0.5),
        'ffn2_w_down': nrm((L, D_FF, D_MODEL), D_FF ** -0.5),
        'final_norm': 1.0 + nrm((D_MODEL,), 0.01),
    }


def reference(x_prompt, x_sample, state_rwkv_shift, state_rwkv_wkv, cache_swa_k, cache_swa_v,
              ffn1_norm, ffn1_w_gate, ffn1_w_up, ffn1_w_down, mix_norm, w_in,
              rwkv_mu, rwkv_decay_up, rwkv_decay_base, rwkv_iclr_up, rwkv_iclr_base, rwkv_gate_up,
              rwkv_k_k, rwkv_k_a, rwkv_r_k, rwkv_lnx_g, rwkv_lnx_b, attn_sinks, w_out,
              ffn2_norm, ffn2_w_gate, ffn2_w_up, ffn2_w_down, final_norm):
    yp, ys = x_prompt, x_sample
    B = x_prompt.shape[0]
    p_shift, p_wkv, p_k, p_v = [], [], [], []
    s_shift, s_wkv, s_k, s_v = [], [], [], []
    for l in range(DEPTH):
        W = {
            'ffn1_norm': ffn1_norm[l], 'ffn1_w_gate': ffn1_w_gate[l], 'ffn1_w_up': ffn1_w_up[l],
            'ffn1_w_down': ffn1_w_down[l], 'mix_norm': mix_norm[l], 'w_in': w_in[l],
            'rwkv_mu': rwkv_mu[l], 'rwkv_decay_up': rwkv_decay_up[l], 'rwkv_decay_base': rwkv_decay_base[l],
            'rwkv_iclr_up': rwkv_iclr_up[l], 'rwkv_iclr_base': rwkv_iclr_base[l], 'rwkv_gate_up': rwkv_gate_up[l],
            'rwkv_k_k': rwkv_k_k[l], 'rwkv_k_a': rwkv_k_a[l], 'rwkv_r_k': rwkv_r_k[l],
            'rwkv_lnx_g': rwkv_lnx_g[l], 'rwkv_lnx_b': rwkv_lnx_b[l], 'attn_sinks': attn_sinks[l],
            'w_out': w_out[l], 'ffn2_norm': ffn2_norm[l], 'ffn2_w_gate': ffn2_w_gate[l],
            'ffn2_w_up': ffn2_w_up[l], 'ffn2_w_down': ffn2_w_down[l],
        }
        shift0 = jnp.zeros((B, SHIFT_WIDTH), yp.dtype)
        wkv0 = jnp.zeros((B, N_RWKV_HEADS, HEAD_DIM, HEAD_DIM), yp.dtype)
        yp, a1, a2, a3, a4 = _layer(yp, shift0, wkv0, None, None, W)
        ys, b1, b2, b3, b4 = _layer(ys, state_rwkv_shift[l], state_rwkv_wkv[l], cache_swa_k[l], cache_swa_v[l], W)
        p_shift.append(a1); p_wkv.append(a2); p_k.append(a3); p_v.append(a4)
        s_shift.append(b1); s_wkv.append(b2); s_k.append(b3); s_v.append(b4)
    y_prompt = _rmsnorm(yp, final_norm)
    y_sample = _rmsnorm(ys, final_norm)
    return (y_prompt, y_sample,
            jnp.stack(p_shift), jnp.stack(p_wkv), jnp.stack(p_k), jnp.stack(p_v),
            jnp.stack(s_shift), jnp.stack(s_wkv), jnp.stack(s_k), jnp.stack(s_v))
```

```python
import functools

import jax
import jax.numpy as jnp
from jax import lax
from jax.experimental import pallas as pl
from jax.experimental.pallas import tpu as pltpu

F32 = jnp.float32
BF16 = jnp.bfloat16
HIGHEST = lax.Precision.HIGHEST

HEAD_DIM = 64
N_HEADS = 16
N_KV_HEADS = 4
GQA_GROUP = 4
RWKV_WIDTH = 1024
ATTN_WIDTH = 1024
KV_WIDTH = 256
WINDOW = 128
LORA_WIDTH = 64 + 64 + 160
LORA_PAD = 384
RMS_EPS = 1e-6
GN_EPS = 64e-5
NEG_INF = -1e30

COL_R, COL_K, COL_V = 0, 1024, 2048
COL_Q = 3072
COL_AK = 4096
COL_AV = 4352
COL_LORA = 4608
IN_PAD = COL_LORA + LORA_PAD

LANES = 128
CHUNK = 64
VMEM_LIMIT = 56 * 1024 * 1024


def _dot(a, b):
    return jnp.dot(a.astype(BF16), b.astype(BF16), preferred_element_type=F32)


def _dot_nt(a, b):
    return lax.dot_general(a.astype(BF16), b.astype(BF16), (((1,), (1,)), ((), ())),
                           preferred_element_type=F32)


def _dot_hi(a, b):
    return jnp.dot(a, b, preferred_element_type=F32, precision=HIGHEST)


def _rms(x, g):
    return x * lax.rsqrt(jnp.mean(x * x, axis=-1, keepdims=True) + RMS_EPS) * g


def _params(*sem):
    return pltpu.CompilerParams(dimension_semantics=sem, vmem_limit_bytes=VMEM_LIMIT)


def _ffn_kernel(x_ref, g_ref, wg_ref, wu_ref, wd_ref, *rest, final):
    if final:
        fg_ref, o_ref, xn_ref, acc_ref = rest
    else:
        o_ref, xn_ref, acc_ref = rest
    f = pl.program_id(1)

    @pl.when(f == 0)
    def _():
        xn_ref[...] = _rms(x_ref[...], g_ref[...]).astype(BF16)
        acc_ref[...] = jnp.zeros_like(acc_ref)

    xn = xn_ref[...]
    hg = jnp.dot(xn, wg_ref[...], preferred_element_type=F32)
    hu = jnp.dot(xn, wu_ref[...], preferred_element_type=F32)
    h = (hg * jax.nn.sigmoid(hg) * hu).astype(BF16)
    acc_ref[...] += jnp.dot(h, wd_ref[...], preferred_element_type=F32)

    @pl.when(f == pl.num_programs(1) - 1)
    def _():
        y = x_ref[...] + 0.5 * acc_ref[...]
        if final:
            y = _rms(y, fg_ref[...])
        o_ref[...] = y


def _ffn(x, norm_g, wg, wu, wd, final_g=None, *, tm, tf):
    M, D = x.shape
    Fp = wg.shape[1]
    final = final_g is not None
    in_specs = [pl.BlockSpec((tm, D), lambda i, f: (i, 0)),
                pl.BlockSpec((1, D), lambda i, f: (0, 0)),
                pl.BlockSpec((D, tf), lambda i, f: (0, f)),
                pl.BlockSpec((D, tf), lambda i, f: (0, f)),
                pl.BlockSpec((tf, D), lambda i, f: (f, 0))]
    args = [x, norm_g.reshape(1, D), wg, wu, wd]
    if final:
        in_specs.append(pl.BlockSpec((1, D), lambda i, f: (0, 0)))
        args.append(final_g.reshape(1, D))
    return pl.pallas_call(
        functools.partial(_ffn_kernel, final=final),
        grid=(M // tm, Fp // tf),
        in_specs=in_specs,
        out_specs=pl.BlockSpec((tm, D), lambda i, f: (i, 0)),
        out_shape=jax.ShapeDtypeStruct((M, D), F32),
        scratch_shapes=[pltpu.VMEM((tm, D), BF16), pltpu.VMEM((tm, D), F32)],
        compiler_params=_params("parallel", "arbitrary"),
        name="ffn",
    )(*args)


def _inproj_kernel(x_ref, g_ref, w_ref, o_ref, xn_ref):
    @pl.when(pl.program_id(1) == 0)
    def _():
        xn_ref[...] = _rms(x_ref[...], g_ref[...]).astype(BF16)

    o_ref[...] = jnp.dot(xn_ref[...], w_ref[...], preferred_element_type=F32)


def _inproj(x, norm_g, w, *, tm, tn):
    M, D = x.shape
    N = w.shape[1]
    return pl.pallas_call(
        _inproj_kernel,
        grid=(M // tm, N // tn),
        in_specs=[pl.BlockSpec((tm, D), lambda i, j: (i, 0)),
                  pl.BlockSpec((1, D), lambda i, j: (0, 0)),
                  pl.BlockSpec((D, tn), lambda i, j: (0, j))],
        out_specs=pl.BlockSpec((tm, tn), lambda i, j: (i, j)),
        out_shape=jax.ShapeDtypeStruct((M, N), F32),
        scratch_shapes=[pltpu.VMEM((tm, D), BF16)],
        compiler_params=_params("parallel", "arbitrary"),
        name="inproj",
    )(x, norm_g.reshape(1, D), w)


def _outproj_kernel(x_ref, oa_ref, ob_ref, w_ref, o_ref):
    half = oa_ref.shape[1]
    o_ref[...] = (x_ref[...]
                  + jnp.dot(oa_ref[...].astype(BF16), w_ref[:half, :], preferred_element_type=F32)
                  + jnp.dot(ob_ref[...].astype(BF16), w_ref[half:, :], preferred_element_type=F32))


def _outproj(x, o_rwkv, o_attn, w, *, tm):
    M, D = x.shape
    Wd = o_rwkv.shape[1]
    return pl.pallas_call(
        _outproj_kernel,
        grid=(M // tm,),
        in_specs=[pl.BlockSpec((tm, D), lambda i: (i, 0)),
                  pl.BlockSpec((tm, Wd), lambda i: (i, 0)),
                  pl.BlockSpec((tm, Wd), lambda i: (i, 0)),
                  pl.BlockSpec(w.shape, lambda i: (0, 0))],
        out_specs=pl.BlockSpec((tm, D), lambda i: (i, 0)),
        out_shape=jax.ShapeDtypeStruct((M, D), F32),
        compiler_params=_params("parallel"),
        name="outproj",
    )(x, o_rwkv, o_attn, w)


def _head_ones():
    r = lax.broadcasted_iota(jnp.int32, (LANES, LANES), 0) // HEAD_DIM
    c = lax.broadcasted_iota(jnp.int32, (LANES, LANES), 1) // HEAD_DIM
    return (r == c).astype(F32)


def _head_sum(x):
    ones = _head_ones()
    parts = [_dot_hi(x[:, j * LANES:(j + 1) * LANES], ones) for j in range(x.shape[1] // LANES)]
    return parts[0] if len(parts) == 1 else jnp.concatenate(parts, axis=-1)


def _rwkv_prep_math(p_r, p_k, p_v, p_l, q_r, q_k, q_v, q_l, w):
    (mu_r, mu_k, mu_v, mu_l, dec_up, dec_base, icl_up, icl_base, gate_up, k_k, k_a) = w
    r = p_r + mu_r * (q_r - p_r)
    k = p_k + mu_k * (q_k - p_k)
    v = p_v + mu_v * (q_v - p_v)
    ul = p_l + mu_l * (q_l - p_l)
    w_log = -jax.nn.softplus(-(dec_base + _dot(jnp.tanh(ul), dec_up))) - 0.5
    lw = -jnp.exp(w_log)
    a = jax.nn.sigmoid(icl_base + _dot(ul, icl_up))
    g = _dot(jax.nn.sigmoid(ul), gate_up)
    kk = k * k_k
    kk = kk / jnp.maximum(jnp.sqrt(_head_sum(kk * kk)), 1e-12)
    k2 = k * (1.0 + (a - 1.0) * k_a)
    return r, lw, k2, v, kk, kk * a, g


def _prep_prompt_kernel(pr, pk, pv, pL, qr, qk, qv, qL, *rest):
    w_refs, out_refs = rest[:11], rest[11:]
    first = pl.program_id(1) == 0
    rows = pr.shape[0]

    def prev_rows(cur_ref, tail_ref):
        cur = cur_ref[...]
        tail = jnp.where(first, 0.0, tail_ref[7:8, :])
        row = lax.broadcasted_iota(jnp.int32, cur.shape, 0)
        return jnp.where(row == 0, tail, pltpu.roll(cur, 1, axis=0))

    outs = _rwkv_prep_math(pr[...], pk[...], pv[...], pL[...],
                           prev_rows(pr, qr), prev_rows(pk, qk), prev_rows(pv, qv), prev_rows(pL, qL),
                           tuple(r[...] for r in w_refs))
    for o_ref, o in zip(out_refs, outs):
        o_ref[...] = o


def _prep_sample_kernel(pr, pk, pv, pL, qr, qk, qv, qL, *rest):
    w_refs, out_refs = rest[:11], rest[11:]
    outs = _rwkv_prep_math(pr[...], pk[...], pv[...], pL[...], qr[...], qk[...], qv[...], qL[...],
                           tuple(r[...] for r in w_refs))
    for o_ref, o in zip(out_refs, outs):
        o_ref[...] = o


def _prep_weight_specs(w, nargs):
    zero = (lambda b, i: (0, 0)) if nargs == 2 else (lambda i: (0, 0))
    return [pl.BlockSpec(a.shape, zero) for a in w]


def _prep_prompt(p_all, w, *, batch, seq, tp):
    nt = seq // tp
    W = RWKV_WIDTH

    def cur(col_block, width):
        return pl.BlockSpec((tp, width), lambda b, i: (b * nt + i, col_block))

    def tail(col_block, width):
        return pl.BlockSpec((8, width), lambda b, i: (jnp.maximum((b * seq + i * tp) // 8 - 1, 0), col_block))

    in_specs = [cur(0, W), cur(1, W), cur(2, W), cur(COL_LORA // LORA_PAD, LORA_PAD),
                tail(0, W), tail(1, W), tail(2, W), tail(COL_LORA // LORA_PAD, LORA_PAD)]
    in_specs += _prep_weight_specs(w, 2)
    out = jax.ShapeDtypeStruct((batch * seq, W), F32)
    return pl.pallas_call(
        _prep_prompt_kernel,
        grid=(batch, nt),
        in_specs=in_specs,
        out_specs=[pl.BlockSpec((tp, W), lambda b, i: (b * nt + i, 0))] * 7,
        out_shape=[out] * 7,
        compiler_params=_params("parallel", "arbitrary"),
        name="rwkv_prep_prompt",
    )(p_all, p_all, p_all, p_all, p_all, p_all, p_all, p_all, *w)


def _prep_sample(p_all, shift_r, shift_k, shift_v, shift_l, w, *, row0, rows):
    W = RWKV_WIDTH
    rb = row0 // rows

    def cur(col_block, width):
        return pl.BlockSpec((rows, width), lambda i: (rb, col_block))

    def full(width):
        return pl.BlockSpec((rows, width), lambda i: (0, 0))

    in_specs = [cur(0, W), cur(1, W), cur(2, W), cur(COL_LORA // LORA_PAD, LORA_PAD),
                full(W), full(W), full(W), full(LORA_PAD)]
    in_specs += _prep_weight_specs(w, 1)
    out = jax.ShapeDtypeStruct((rows, W), F32)
    return pl.pallas_call(
        _prep_sample_kernel,
        grid=(1,),
        in_specs=in_specs,
        out_specs=[pl.BlockSpec((rows, W), lambda i: (0, 0))] * 7,
        out_shape=[out] * 7,
        compiler_params=_params("arbitrary"),
        name="rwkv_prep_sample",
    )(p_all, p_all, p_all, p_all, shift_r, shift_k, shift_v, shift_l, *w)


def _stack_heads(x):
    lane = lax.broadcasted_iota(jnp.int32, x.shape, 1)
    return jnp.concatenate([jnp.where(lane < HEAD_DIM, x, 0.0), jnp.where(lane >= HEAD_DIM, x, 0.0)], axis=0)


def _rwkv_scan_kernel(r_ref, lw_ref, k_ref, v_ref, kk_ref, b_ref, g_ref, rk_ref, lng_ref, lnb_ref,
                      o_ref, s_out_ref, st_ref):
    C = CHUNK
    n_chunks = r_ref.shape[0] // C

    @pl.when(pl.program_id(2) == 0)
    def _():
        st_ref[...] = jnp.zeros_like(st_ref)

    row = lax.broadcasted_iota(jnp.int32, (2 * C, 2 * C), 0)
    col = lax.broadcasted_iota(jnp.int32, (2 * C, 2 * C), 1)
    strict = col < row
    incl = col <= row
    eye = (col == row).astype(F32)
    tri = (lax.broadcasted_iota(jnp.int32, (C, C), 1) <= lax.broadcasted_iota(jnp.int32, (C, C), 0)).astype(F32)
    head_mean = _head_ones() * (1.0 / HEAD_DIM)
    rk, lng, lnb = rk_ref[...], lng_ref[...], lnb_ref[...]

    def chunk(c, carry):
        sl = pl.ds(pl.multiple_of(c * C, C), C)
        r, lw, k, v, kk, b, g = (ref[sl, :] for ref in (r_ref, lw_ref, k_ref, v_ref, kk_ref, b_ref, g_ref))
        st = st_ref[...]
        cum = _dot_hi(tri, lw)
        total = cum[C - 1:C, :]
        rg = _stack_heads(r * jnp.exp(cum))
        nag = _stack_heads(-kk * jnp.exp(cum - lw))
        ginv = jnp.exp(-cum)
        bi = _stack_heads(b * ginv)
        ki = _stack_heads(k * ginv)
        gend = jnp.exp(total - cum)
        bend = _stack_heads(b * gend)
        kend = _stack_heads(k * gend)
        vs = _stack_heads(v)

        gram = _dot_nt(jnp.concatenate([nag, rg], axis=0), jnp.concatenate([bi, ki], axis=0))
        a_ab = jnp.where(strict, gram[:2 * C, :2 * C], 0.0)
        a_ak = jnp.where(strict, gram[:2 * C, 2 * C:], 0.0)
        a_rb = jnp.where(incl, gram[2 * C:, :2 * C], 0.0)
        a_rk = jnp.where(incl, gram[2 * C:, 2 * C:], 0.0)

        pw = a_ab
        inv = eye + pw
        span = 1
        while span * 2 < C:
            pw = _dot(pw, pw)
            inv = inv + _dot(inv, pw)
            span *= 2

        z = _dot_nt(nag, st) + _dot(a_ak, vs)
        u = _dot(inv, z)
        o2 = _dot_nt(rg, st) + _dot(a_rb, u) + _dot(a_rk, vs)
        o = o2[:C, :] + o2[C:, :]
        st_ref[...] = st * jnp.exp(total) + _dot(u.T, bend) + _dot(vs.T, kend)

        mean = _dot_hi(o, head_mean)
        d = o - mean
        var = _dot_hi(d * d, head_mean)
        on = d * lax.rsqrt(var + GN_EPS) * lng + lnb
        bonus = _head_sum(r * k * rk) * v
        o_ref[sl, :] = (on + bonus) * g
        return carry

    lax.fori_loop(0, n_chunks, chunk, 0)

    @pl.when(pl.program_id(2) == pl.num_programs(2) - 1)
    def _():
        st = st_ref[...]
        s_out_ref[0, 0] = st[:HEAD_DIM, :HEAD_DIM]
        s_out_ref[0, 1] = st[HEAD_DIM:, HEAD_DIM:]


def _rwkv_scan(r, lw, k, v, kk, b, g, r_k, lnx_g, lnx_b, *, batch, seq, tb):
    nt = seq // tb
    pairs = RWKV_WIDTH // LANES
    tok = pl.BlockSpec((tb, LANES), lambda bi, j, i: (bi * nt + i, j))
    par = pl.BlockSpec((1, LANES), lambda bi, j, i: (0, j))
    return pl.pallas_call(
        _rwkv_scan_kernel,
        grid=(batch, pairs, nt),
        in_specs=[tok] * 7 + [par] * 3,
        out_specs=[tok, pl.BlockSpec((1, 2, HEAD_DIM, HEAD_DIM), lambda bi, j, i: (bi, j, 0, 0))],
        out_shape=[jax.ShapeDtypeStruct((batch * seq, RWKV_WIDTH), F32),
                   jax.ShapeDtypeStruct((batch, N_HEADS, HEAD_DIM, HEAD_DIM), F32)],
        scratch_shapes=[pltpu.VMEM((LANES, LANES), F32)],
        compiler_params=_params("parallel", "parallel", "arbitrary"),
        name="rwkv_scan",
    )(r, lw, k, v, kk, b, g, r_k, lnx_g, lnx_b)


def _rwkv_step_kernel(s_ref, r_ref, lw_ref, k_ref, v_ref, kk_ref, b_ref, g_ref, rk_ref, lng_ref, lnb_ref,
                      o_ref, s_out_ref):
    s = s_ref[...]
    r, k, v, kk, b, g = (ref[...] for ref in (r_ref, k_ref, v_ref, kk_ref, b_ref, g_ref))
    w = jnp.exp(lw_ref[...])
    shape = (HEAD_DIM, HEAD_DIM)
    eye = lax.broadcasted_iota(jnp.int32, shape, 0) == lax.broadcasted_iota(jnp.int32, shape, 1)
    sa = jnp.sum(s * (-kk), axis=-1, keepdims=True)
    v_col = jnp.sum(jnp.where(eye, v, 0.0), axis=-1, keepdims=True)
    s_new = s * w + sa * b + v_col * k
    s_out_ref[...] = s_new
    o_col = jnp.sum(s_new * r, axis=-1, keepdims=True)
    o = jnp.sum(jnp.where(eye, o_col, 0.0), axis=-2, keepdims=True)
    mean = jnp.mean(o, axis=-1, keepdims=True)
    d = o - mean
    var = jnp.mean(d * d, axis=-1, keepdims=True)
    on = d * lax.rsqrt(var + GN_EPS) * lng_ref[...] + lnb_ref[...]
    bonus = jnp.sum(r * k * rk_ref[...], axis=-1, keepdims=True) * v
    o_ref[...] = (on + bonus) * g


def _rwkv_step(state, r, lw, k, v, kk, b, g, r_k, lnx_g, lnx_b, *, bt):
    B = state.shape[0]
    vec4 = lambda a: a.reshape(B, N_HEADS, 1, HEAD_DIM)
    par4 = lambda a: a.reshape(1, N_HEADS, 1, HEAD_DIM)
    st_spec = pl.BlockSpec((bt, N_HEADS, HEAD_DIM, HEAD_DIM), lambda i: (i, 0, 0, 0))
    vec_spec = pl.BlockSpec((bt, N_HEADS, 1, HEAD_DIM), lambda i: (i, 0, 0, 0))
    par_spec = pl.BlockSpec((1, N_HEADS, 1, HEAD_DIM), lambda i: (0, 0, 0, 0))
    o, s_new = pl.pallas_call(
        _rwkv_step_kernel,
        grid=(B // bt,),
        in_specs=[st_spec] + [vec_spec] * 7 + [par_spec] * 3,
        out_specs=[vec_spec, st_spec],
        out_shape=[jax.ShapeDtypeStruct((B, N_HEADS, 1, HEAD_DIM), F32),
                   jax.ShapeDtypeStruct(state.shape, F32)],
        compiler_params=_params("parallel"),
        name="rwkv_step",
    )(state, *(vec4(a) for a in (r, lw, k, v, kk, b, g)), par4(r_k), par4(lnx_g), par4(lnx_b))
    return o.reshape(B, RWKV_WIDTH), s_new


def _swa_prompt_kernel(q_ref, kc_ref, kp_ref, vc_ref, vp_ref, slope_ref, sink_ref, o_ref):
    blk = q_ref.shape[0]
    has_prev = pl.program_id(2) > 0
    qi = lax.broadcasted_iota(jnp.int32, (blk, 2 * blk), 0) + blk
    kj = lax.broadcasted_iota(jnp.int32, (blk, 2 * blk), 1)
    dist = qi - kj
    valid = (dist >= 0) & (dist < WINDOW) & (has_prev | (kj >= blk))
    distf = dist.astype(F32)
    for kvh in range(2):
        lanes = slice(kvh * HEAD_DIM, (kvh + 1) * HEAD_DIM)
        keys = jnp.concatenate([kp_ref[:, lanes], kc_ref[:, lanes]], axis=0).astype(BF16)
        vals = jnp.concatenate([vp_ref[:, lanes], vc_ref[:, lanes]], axis=0).astype(BF16)
        for gq in range(GQA_GROUP):
            h = kvh * GQA_GROUP + gq
            cols = slice(h * HEAD_DIM, (h + 1) * HEAD_DIM)
            s = _dot_nt(q_ref[:, cols], keys) * (HEAD_DIM ** -0.5)
            s = jnp.where(valid, s - slope_ref[h] * distf, NEG_INF)
            sink = sink_ref[h][:, :1]
            m = jnp.maximum(jnp.max(s, axis=-1, keepdims=True), sink)
            e = jnp.exp(s - m)
            denom = jnp.sum(e, axis=-1, keepdims=True) + jnp.exp(sink - m)
            o_ref[:, cols] = _dot(e, vals) / denom


def _swa_prompt(p_all, slopes, sinks, *, batch, seq):
    blk = WINDOW
    nb = seq // blk
    qw = 2 * GQA_GROUP * HEAD_DIM
    cur = lambda col: (lambda b, j, i: (b * nb + i, col + j))
    prev = lambda col: (lambda b, j, i: (b * nb + jnp.maximum(i - 1, 0), col + j))
    par = pl.BlockSpec((2 * GQA_GROUP, 1, 2 * blk), lambda b, j, i: (j, 0, 0))
    return pl.pallas_call(
        _swa_prompt_kernel,
        grid=(batch, N_KV_HEADS // 2, nb),
        in_specs=[pl.BlockSpec((blk, qw), cur(COL_Q // qw)),
                  pl.BlockSpec((blk, LANES), cur(COL_AK // LANES)),
                  pl.BlockSpec((blk, LANES), prev(COL_AK // LANES)),
                  pl.BlockSpec((blk, LANES), cur(COL_AV // LANES)),
                  pl.BlockSpec((blk, LANES), prev(COL_AV // LANES)),
                  par, par],
        out_specs=pl.BlockSpec((blk, qw), lambda b, j, i: (b * nb + i, j)),
        out_shape=jax.ShapeDtypeStruct((batch * seq, ATTN_WIDTH), F32),
        compiler_params=_params("parallel", "parallel", "arbitrary"),
        name="swa_prompt",
    )(p_all, p_all, p_all, p_all, p_all,
      jnp.broadcast_to(slopes.reshape(N_HEADS, 1, 1), (N_HEADS, 1, 2 * blk)),
      jnp.broadcast_to(sinks.reshape(N_HEADS, 1, 1), (N_HEADS, 1, 2 * blk)))


def _swa_sample_kernel(q_ref, kn_ref, vn_ref, ck_ref, cv_ref, slope_ref, sink_ref, o_ref, kw_ref, vw_ref):
    bt = q_ref.shape[0]
    win = ck_ref.shape[1]
    row = lax.broadcasted_iota(jnp.int32, (win, KV_WIDTH), 0)
    head_kv = lax.broadcasted_iota(jnp.int32, (N_HEADS, KV_WIDTH), 0) // GQA_GROUP
    lane_kv = lax.broadcasted_iota(jnp.int32, (N_HEADS, KV_WIDTH), 1) // HEAD_DIM
    own = head_kv == lane_kv
    dist = (win - 1 - lax.broadcasted_iota(jnp.int32, (N_HEADS, win), 1)).astype(F32)
    bias = slope_ref[...] * dist
    sink = sink_ref[...][:, :1]
    for i in range(bt):
        keys = jnp.where(row == win - 1, kn_ref[i:i + 1, :], pltpu.roll(ck_ref[i], win - 1, axis=0))
        vals = jnp.where(row == win - 1, vn_ref[i:i + 1, :], pltpu.roll(cv_ref[i], win - 1, axis=0))
        kw_ref[i] = keys
        vw_ref[i] = vals
        q = q_ref[i]
        qbd = jnp.where(own, jnp.concatenate([q] * N_KV_HEADS, axis=-1), 0.0)
        s = _dot_nt(qbd, keys) * (HEAD_DIM ** -0.5) - bias
        m = jnp.maximum(jnp.max(s, axis=-1, keepdims=True), sink)
        e = jnp.exp(s - m)
        denom = jnp.sum(e, axis=-1, keepdims=True) + jnp.exp(sink - m)
        o2 = jnp.where(own, _dot(e, vals), 0.0)
        o = o2[:, :HEAD_DIM]
        for c in range(1, N_KV_HEADS):
            o = o + o2[:, c * HEAD_DIM:(c + 1) * HEAD_DIM]
        o_ref[i] = o / denom


def _swa_sample(q, k_new, v_new, cache_k, cache_v, slopes, sinks, *, bt):
    B, win = cache_k.shape[0], cache_k.shape[1]
    q3 = pl.BlockSpec((bt, N_HEADS, HEAD_DIM), lambda i: (i, 0, 0))
    new = pl.BlockSpec((bt, KV_WIDTH), lambda i: (i, 0))
    cache = pl.BlockSpec((bt, win, KV_WIDTH), lambda i: (i, 0, 0))
    par = pl.BlockSpec((N_HEADS, win), lambda i: (0, 0))
    o, kw, vw = pl.pallas_call(
        _swa_sample_kernel,
        grid=(B // bt,),
        in_specs=[q3, new, new, cache, cache, par, par],
        out_specs=[q3, cache, cache],
        out_shape=[jax.ShapeDtypeStruct((B, N_HEADS, HEAD_DIM), F32),
                   jax.ShapeDtypeStruct((B, win, KV_WIDTH), F32),
                   jax.ShapeDtypeStruct((B, win, KV_WIDTH), F32)],
        compiler_params=_params("parallel"),
        name="swa_sample",
    )(q.reshape(B, N_HEADS, HEAD_DIM), k_new, v_new,
      cache_k.reshape(B, win, KV_WIDTH), cache_v.reshape(B, win, KV_WIDTH),
      jnp.broadcast_to(slopes.reshape(N_HEADS, 1), (N_HEADS, win)),
      jnp.broadcast_to(sinks.reshape(N_HEADS, 1), (N_HEADS, win)))
    return o.reshape(B, ATTN_WIDTH), kw, vw


def _permute_in_cols(a):
    attn_proj = ATTN_WIDTH + 2 * KV_WIDTH
    rkv = a[..., attn_proj:attn_proj + 3 * RWKV_WIDTH]
    lora = a[..., attn_proj + 3 * RWKV_WIDTH:]
    pad = jnp.zeros(a.shape[:-1] + (LORA_PAD - LORA_WIDTH,), a.dtype)
    return jnp.concatenate([rkv, a[..., :attn_proj], lora, pad], axis=-1)


def _lora_rows(w, start):
    return jnp.zeros((LORA_PAD, w.shape[1]), F32).at[start:start + w.shape[0]].set(w).astype(BF16)


def _pad_ff(w, axis, to):
    pad = [(0, 0), (0, 0)]
    pad[axis] = (0, to - w.shape[axis])
    return jnp.pad(w, pad).astype(BF16)


def kernel(x_prompt, x_sample, state_rwkv_shift, state_rwkv_wkv, cache_swa_k, cache_swa_v, ffn1_norm, ffn1_w_gate, ffn1_w_up, ffn1_w_down, mix_norm, w_in, rwkv_mu, rwkv_decay_up, rwkv_decay_base, rwkv_iclr_up, rwkv_iclr_base, rwkv_gate_up, rwkv_k_k, rwkv_k_a, rwkv_r_k, rwkv_lnx_g, rwkv_lnx_b, attn_sinks, w_out, ffn2_norm, ffn2_w_gate, ffn2_w_up, ffn2_w_down, final_norm):
    batch, seq, d_model = x_prompt.shape
    dec_batch = x_sample.shape[0]
    depth = ffn1_norm.shape[0]
    n_prompt = batch * seq
    d_ff = ffn1_w_gate.shape[-1]
    ff_pad = -(-d_ff // 512) * 512
    win = cache_swa_k.shape[2]
    assert x_sample.shape[1] == 1 and seq % 512 == 0 and win == WINDOW and n_prompt % dec_batch == 0
    tm = 640
    assert (n_prompt + dec_batch) % tm == 0

    x = jnp.concatenate([x_prompt.reshape(n_prompt, d_model), x_sample.reshape(dec_batch, d_model)], axis=0)
    slopes = jnp.exp2(-8.0 * jnp.arange(1, N_HEADS + 1, dtype=F32) / N_HEADS)
    outs = [[] for _ in range(8)]
    for l in range(depth):
        x = _ffn(x, ffn1_norm[l], _pad_ff(ffn1_w_gate[l], 1, ff_pad), _pad_ff(ffn1_w_up[l], 1, ff_pad),
                 _pad_ff(ffn1_w_down[l], 0, ff_pad), tm=tm, tf=512)
        p_all = _inproj(x, mix_norm[l], _permute_in_cols(w_in[l]).astype(BF16), tm=tm, tn=IN_PAD // 3)

        mu = _permute_in_cols(jnp.concatenate([jnp.zeros((ATTN_WIDTH + 2 * KV_WIDTH,), F32), rwkv_mu[l]]))
        row = lambda a: a.reshape(1, -1)
        prep_w = (row(mu[COL_R:COL_K]), row(mu[COL_K:COL_V]), row(mu[COL_V:COL_Q]), row(mu[COL_LORA:]),
                  _lora_rows(rwkv_decay_up[l], 0), row(rwkv_decay_base[l]),
                  _lora_rows(rwkv_iclr_up[l], 64), row(rwkv_iclr_base[l]),
                  _lora_rows(rwkv_gate_up[l], 128), row(rwkv_k_k[l]), row(rwkv_k_a[l]))
        head_par = (row(rwkv_r_k[l]), row(rwkv_lnx_g[l]), row(rwkv_lnx_b[l]))

        prep = _prep_prompt(p_all, prep_w, batch=batch, seq=seq, tp=256)
        o_rwkv_p, wkv_p = _rwkv_scan(*prep, *head_par, batch=batch, seq=seq, tb=512)
        o_attn_p = _swa_prompt(p_all, slopes, attn_sinks[l], batch=batch, seq=seq)

        sh = _permute_in_cols(jnp.concatenate(
            [jnp.zeros((dec_batch, ATTN_WIDTH + 2 * KV_WIDTH), F32), state_rwkv_shift[l]], axis=-1))
        prep_s = _prep_sample(p_all, sh[:, COL_R:COL_K], sh[:, COL_K:COL_V], sh[:, COL_V:COL_Q], sh[:, COL_LORA:],
                              prep_w, row0=n_prompt, rows=dec_batch)
        o_rwkv_s, wkv_s = _rwkv_step(state_rwkv_wkv[l], *prep_s, *head_par, bt=8)
        p_s = p_all[n_prompt:]
        o_attn_s, kwin_s, vwin_s = _swa_sample(p_s[:, COL_Q:COL_AK], p_s[:, COL_AK:COL_AV], p_s[:, COL_AV:COL_LORA],
                                               cache_swa_k[l], cache_swa_v[l], slopes, attn_sinks[l], bt=8)

        x = _outproj(x, jnp.concatenate([o_rwkv_p, o_rwkv_s], axis=0), jnp.concatenate([o_attn_p, o_attn_s], axis=0),
                     w_out[l].astype(BF16), tm=tm)
        x = _ffn(x, ffn2_norm[l], _pad_ff(ffn2_w_gate[l], 1, ff_pad), _pad_ff(ffn2_w_up[l], 1, ff_pad),
                 _pad_ff(ffn2_w_down[l], 0, ff_pad), final_norm if l == depth - 1 else None, tm=tm, tf=512)

        last = p_all.reshape(-1, 1, IN_PAD)[seq - 1:n_prompt:seq, 0]
        shift_cols = lambda a: jnp.concatenate([a[:, COL_R:COL_Q], a[:, COL_LORA:COL_LORA + LORA_WIDTH]], axis=-1)
        p_p = p_all[:n_prompt].reshape(batch, seq, IN_PAD)
        wb = min(WINDOW, seq)
        for lst, val in zip(outs, (
                shift_cols(last), wkv_p,
                p_p[:, seq - wb:, COL_AK:COL_AV].reshape(batch, wb, N_KV_HEADS, HEAD_DIM),
                p_p[:, seq - wb:, COL_AV:COL_LORA].reshape(batch, wb, N_KV_HEADS, HEAD_DIM),
                shift_cols(p_s), wkv_s,
                kwin_s.reshape(dec_batch, win, N_KV_HEADS, HEAD_DIM),
                vwin_s.reshape(dec_batch, win, N_KV_HEADS, HEAD_DIM))):
            lst.append(val)

    y_prompt = x[:n_prompt].reshape(batch, seq, d_model)
    y_sample = x[n_prompt:].reshape(dec_batch, 1, d_model)
    return (y_prompt, y_sample) + tuple(jnp.stack(o) for o in outs)
```

```python
import functools

import jax
import jax.numpy as jnp
from jax import lax
from jax.experimental import pallas as pl
from jax.experimental.pallas import tpu as pltpu

F32 = jnp.float32
BF16 = jnp.bfloat16
HIGHEST = lax.Precision.HIGHEST

HEAD_DIM = 64
N_HEADS = 16
N_KV_HEADS = 4
GQA_GROUP = 4
RWKV_WIDTH = 1024
ATTN_WIDTH = 1024
KV_WIDTH = 256
WINDOW = 128
LORA_WIDTH = 64 + 64 + 160
LORA_PAD = 384
RMS_EPS = 1e-6
GN_EPS = 64e-5
NEG_INF = -1e30

COL_R, COL_K, COL_V = 0, 1024, 2048
COL_Q = 3072
COL_AK = 4096
COL_AV = 4352
COL_LORA = 4608
IN_PAD = COL_LORA + LORA_PAD

LANES = 128
CHUNK = 64
VMEM_LIMIT = 56 * 1024 * 1024


def _dot(a, b):
    return jnp.dot(a.astype(BF16), b.astype(BF16), preferred_element_type=F32)


def _dot_nt(a, b):
    return lax.dot_general(a.astype(BF16), b.astype(BF16), (((1,), (1,)), ((), ())),
                           preferred_element_type=F32)


def _dot_hi(a, b):
    return jnp.dot(a, b, preferred_element_type=F32, precision=HIGHEST)


def _rms(x, g):
    return x * lax.rsqrt(jnp.mean(x * x, axis=-1, keepdims=True) + RMS_EPS) * g


def _params(*sem):
    return pltpu.CompilerParams(dimension_semantics=sem, vmem_limit_bytes=VMEM_LIMIT)


def _ffn_kernel(x_ref, g_ref, wg_ref, wu_ref, wd_ref, *rest, final):
    if final:
        fg_ref, o_ref, xn_ref, acc_ref = rest
    else:
        o_ref, xn_ref, acc_ref = rest
    f = pl.program_id(1)

    @pl.when(f == 0)
    def _():
        xn_ref[...] = _rms(x_ref[...], g_ref[...]).astype(BF16)
        acc_ref[...] = jnp.zeros_like(acc_ref)

    xn = xn_ref[...]
    hg = jnp.dot(xn, wg_ref[...], preferred_element_type=F32)
    hu = jnp.dot(xn, wu_ref[...], preferred_element_type=F32)
    h = (hg * jax.nn.sigmoid(hg) * hu).astype(BF16)
    acc_ref[...] += jnp.dot(h, wd_ref[...], preferred_element_type=F32)

    @pl.when(f == pl.num_programs(1) - 1)
    def _():
        y = x_ref[...] + 0.5 * acc_ref[...]
        if final:
            y = _rms(y, fg_ref[...])
        o_ref[...] = y


def _ffn(x, norm_g, wg, wu, wd, final_g=None, *, tm, tf):
    M, D = x.shape
    Fp = wg.shape[1]
    final = final_g is not None
    in_specs = [pl.BlockSpec((tm, D), lambda i, f: (i, 0)),
                pl.BlockSpec((1, D), lambda i, f: (0, 0)),
                pl.BlockSpec((D, tf), lambda i, f: (0, f)),
                pl.BlockSpec((D, tf), lambda i, f: (0, f)),
                pl.BlockSpec((tf, D), lambda i, f: (f, 0))]
    args = [x, norm_g.reshape(1, D), wg, wu, wd]
    if final:
        in_specs.append(pl.BlockSpec((1, D), lambda i, f: (0, 0)))
        args.append(final_g.reshape(1, D))
    return pl.pallas_call(
        functools.partial(_ffn_kernel, final=final),
        grid=(M // tm, Fp // tf),
        in_specs=in_specs,
        out_specs=pl.BlockSpec((tm, D), lambda i, f: (i, 0)),
        out_shape=jax.ShapeDtypeStruct((M, D), F32),
        scratch_shapes=[pltpu.VMEM((tm, D), BF16), pltpu.VMEM((tm, D), F32)],
        compiler_params=_params("parallel", "arbitrary"),
        name="ffn",
    )(*args)


def _inproj_kernel(x_ref, g_ref, w_ref, o_ref, xn_ref):
    @pl.when(pl.program_id(1) == 0)
    def _():
        xn_ref[...] = _rms(x_ref[...], g_ref[...]).astype(BF16)

    o_ref[...] = jnp.dot(xn_ref[...], w_ref[...], preferred_element_type=F32)


def _inproj(x, norm_g, w, *, tm, tn):
    M, D = x.shape
    N = w.shape[1]
    return pl.pallas_call(
        _inproj_kernel,
        grid=(M // tm, N // tn),
        in_specs=[pl.BlockSpec((tm, D), lambda i, j: (i, 0)),
                  pl.BlockSpec((1, D), lambda i, j: (0, 0)),
                  pl.BlockSpec((D, tn), lambda i, j: (0, j))],
        out_specs=pl.BlockSpec((tm, tn), lambda i, j: (i, j)),
        out_shape=jax.ShapeDtypeStruct((M, N), F32),
        scratch_shapes=[pltpu.VMEM((tm, D), BF16)],
        compiler_params=_params("parallel", "arbitrary"),
        name="inproj",
    )(x, norm_g.reshape(1, D), w)


def _outproj_kernel(x_ref, oa_ref, ob_ref, w_ref, o_ref):
    half = oa_ref.shape[1]
    o_ref[...] = (x_ref[...]
                  + jnp.dot(oa_ref[...].astype(BF16), w_ref[:half, :], preferred_element_type=F32)
                  + jnp.dot(ob_ref[...].astype(BF16), w_ref[half:, :], preferred_element_type=F32))


def _outproj(x, o_rwkv, o_attn, w, *, tm):
    M, D = x.shape
    Wd = o_rwkv.shape[1]
    return pl.pallas_call(
        _outproj_kernel,
        grid=(M // tm,),
        in_specs=[pl.BlockSpec((tm, D), lambda i: (i, 0)),
                  pl.BlockSpec((tm, Wd), lambda i: (i, 0)),
                  pl.BlockSpec((tm, Wd), lambda i: (i, 0)),
                  pl.BlockSpec(w.shape, lambda i: (0, 0))],
        out_specs=pl.BlockSpec((tm, D), lambda i: (i, 0)),
        out_shape=jax.ShapeDtypeStruct((M, D), F32),
        compiler_params=_params("parallel"),
        name="outproj",
    )(x, o_rwkv, o_attn, w)


def _head_ones():
    r = lax.broadcasted_iota(jnp.int32, (LANES, LANES), 0) // HEAD_DIM
    c = lax.broadcasted_iota(jnp.int32, (LANES, LANES), 1) // HEAD_DIM
    return (r == c).astype(F32)


def _head_sum(x):
    ones = _head_ones()
    parts = [_dot_hi(x[:, j * LANES:(j + 1) * LANES], ones) for j in range(x.shape[1] // LANES)]
    return parts[0] if len(parts) == 1 else jnp.concatenate(parts, axis=-1)


def _rwkv_prep_math(p_r, p_k, p_v, p_l, q_r, q_k, q_v, q_l, w):
    (mu_r, mu_k, mu_v, mu_l, dec_up, dec_base, icl_up, icl_base, gate_up, k_k, k_a) = w
    r = p_r + mu_r * (q_r - p_r)
    k = p_k + mu_k * (q_k - p_k)
    v = p_v + mu_v * (q_v - p_v)
    ul = p_l + mu_l * (q_l - p_l)
    w_log = -jax.nn.softplus(-(dec_base + _dot(jnp.tanh(ul), dec_up))) - 0.5
    lw = -jnp.exp(w_log)
    a = jax.nn.sigmoid(icl_base + _dot(ul, icl_up))
    g = _dot(jax.nn.sigmoid(ul), gate_up)
    kk = k * k_k
    kk = kk / jnp.maximum(jnp.sqrt(_head_sum(kk * kk)), 1e-12)
    k2 = k * (1.0 + (a - 1.0) * k_a)
    return r, lw, k2, v, kk, kk * a, g


def _prep_prompt_kernel(pr, pk, pv, pL, qr, qk, qv, qL, *rest):
    w_refs, out_refs = rest[:11], rest[11:]
    first = pl.program_id(1) == 0
    rows = pr.shape[0]

    def prev_rows(cur_ref, tail_ref):
        cur = cur_ref[...]
        tail = jnp.where(first, 0.0, tail_ref[7:8, :])
        row = lax.broadcasted_iota(jnp.int32, cur.shape, 0)
        return jnp.where(row == 0, tail, pltpu.roll(cur, 1, axis=0))

    outs = _rwkv_prep_math(pr[...], pk[...], pv[...], pL[...],
                           prev_rows(pr, qr), prev_rows(pk, qk), prev_rows(pv, qv), prev_rows(pL, qL),
                           tuple(r[...] for r in w_refs))
    for o_ref, o in zip(out_refs, outs):
        o_ref[...] = o


def _prep_sample_kernel(pr, pk, pv, pL, qr, qk, qv, qL, *rest):
    w_refs, out_refs = rest[:11], rest[11:]
    outs = _rwkv_prep_math(pr[...], pk[...], pv[...], pL[...], qr[...], qk[...], qv[...], qL[...],
                           tuple(r[...] for r in w_refs))
    for o_ref, o in zip(out_refs, outs):
        o_ref[...] = o


def _prep_weight_specs(w, nargs):
    zero = (lambda b, i: (0, 0)) if nargs == 2 else (lambda i: (0, 0))
    return [pl.BlockSpec(a.shape, zero) for a in w]


def _prep_prompt(p_all, w, *, batch, seq, tp):
    nt = seq // tp
    W = RWKV_WIDTH

    def cur(col_block, width):
        return pl.BlockSpec((tp, width), lambda b, i: (b * nt + i, col_block))

    def tail(col_block, width):
        return pl.BlockSpec((8, width), lambda b, i: (jnp.maximum((b * seq + i * tp) // 8 - 1, 0), col_block))

    in_specs = [cur(0, W), cur(1, W), cur(2, W), cur(COL_LORA // LORA_PAD, LORA_PAD),
                tail(0, W), tail(1, W), tail(2, W), tail(COL_LORA // LORA_PAD, LORA_PAD)]
    in_specs += _prep_weight_specs(w, 2)
    out = jax.ShapeDtypeStruct((batch * seq, W), F32)
    return pl.pallas_call(
        _prep_prompt_kernel,
        grid=(batch, nt),
        in_specs=in_specs,
        out_specs=[pl.BlockSpec((tp, W), lambda b, i: (b * nt + i, 0))] * 7,
        out_shape=[out] * 7,
        compiler_params=_params("parallel", "arbitrary"),
        name="rwkv_prep_prompt",
    )(p_all, p_all, p_all, p_all, p_all, p_all, p_all, p_all, *w)


def _prep_sample(p_all, shift_r, shift_k, shift_v, shift_l, w, *, row0, rows):
    W = RWKV_WIDTH
    rb = row0 // rows

    def cur(col_block, width):
        return pl.BlockSpec((rows, width), lambda i: (rb, col_block))

    def full(width):
        return pl.BlockSpec((rows, width), lambda i: (0, 0))

    in_specs = [cur(0, W), cur(1, W), cur(2, W), cur(COL_LORA // LORA_PAD, LORA_PAD),
                full(W), full(W), full(W), full(LORA_PAD)]
    in_specs += _prep_weight_specs(w, 1)
    out = jax.ShapeDtypeStruct((rows, W), F32)
    return pl.pallas_call(
        _prep_sample_kernel,
        grid=(1,),
        in_specs=in_specs,
        out_specs=[pl.BlockSpec((rows, W), lambda i: (0, 0))] * 7,
        out_shape=[out] * 7,
        compiler_params=_params("arbitrary"),
        name="rwkv_prep_sample",
    )(p_all, p_all, p_all, p_all, shift_r, shift_k, shift_v, shift_l, *w)


def _stack_heads(x):
    lane = lax.broadcasted_iota(jnp.int32, x.shape, 1)
    return jnp.concatenate([jnp.where(lane < HEAD_DIM, x, 0.0), jnp.where(lane >= HEAD_DIM, x, 0.0)], axis=0)


def _split_terms(x, n):
    terms = []
    for _ in range(n):
        hi = x.astype(BF16)
        terms.append(hi)
        x = x - hi.astype(F32)
    return terms


def _rwkv_chunk_pair(r, lw, k, v, kk, b, g, rk, lng, lnb, st, consts):
    C = CHUNK
    strict, incl, eye, tri, ones = consts
    cum = sum(jnp.dot(tri, t, preferred_element_type=F32) for t in _split_terms(lw, 3))
    yield
    total = cum[C - 1:C, :]
    rg = _stack_heads(r * jnp.exp(cum))
    nag = _stack_heads(-kk * jnp.exp(cum - lw))
    ginv = jnp.exp(-cum)
    bi = _stack_heads(b * ginv)
    ki = _stack_heads(k * ginv)
    gend = jnp.exp(total - cum)
    bend = _stack_heads(b * gend)
    kend = _stack_heads(k * gend)
    vs = _stack_heads(v)

    gram = _dot_nt(jnp.concatenate([nag, rg], axis=0), jnp.concatenate([bi, ki], axis=0))
    yield
    a_ab = jnp.where(strict, gram[:2 * C, :2 * C], 0.0)
    a_ak = jnp.where(strict, gram[:2 * C, 2 * C:], 0.0)
    a_rb = jnp.where(incl, gram[2 * C:, :2 * C], 0.0)
    a_rk = jnp.where(incl, gram[2 * C:, 2 * C:], 0.0)

    def head_sum(x):
        return sum(jnp.dot(t, ones, preferred_element_type=F32) for t in _split_terms(x, 2))

    z = _dot_nt(nag, st) + _dot(a_ak, vs)
    o2 = _dot_nt(rg, st) + _dot(a_rk, vs)
    bonus = head_sum(r * k * rk) * v
    yield

    pw = a_ab
    inv = eye + pw
    span = 1
    while span * 2 < C:
        pw = _dot(pw, pw)
        yield
        inv = inv + _dot(inv, pw)
        span *= 2
    yield

    u = _dot(inv, z)
    yield
    o2 = o2 + _dot(a_rb, u)
    st_new = st * jnp.exp(total) + _dot(u.T, bend) + _dot(vs.T, kend)
    yield
    o = o2[:C, :] + o2[C:, :]
    mean = head_sum(o) * (1.0 / HEAD_DIM)
    yield
    d = o - mean
    var = head_sum(d * d) * (1.0 / HEAD_DIM)
    yield
    on = d * lax.rsqrt(var + GN_EPS) * lng + lnb
    yield (on + bonus) * g, st_new


def _rwkv_scan_kernel(r_ref, lw_ref, k_ref, v_ref, kk_ref, b_ref, g_ref, rk_ref, lng_ref, lnb_ref,
                      o_ref, s_out_ref, st_ref):
    C = CHUNK
    n_chunks = r_ref.shape[0] // C
    pairs = r_ref.shape[1] // LANES

    @pl.when(pl.program_id(1) == 0)
    def _():
        st_ref[...] = jnp.zeros_like(st_ref)

    row = lax.broadcasted_iota(jnp.int32, (2 * C, 2 * C), 0)
    col = lax.broadcasted_iota(jnp.int32, (2 * C, 2 * C), 1)
    tri = lax.broadcasted_iota(jnp.int32, (C, C), 1) <= lax.broadcasted_iota(jnp.int32, (C, C), 0)
    consts = (col < row, col <= row, (col == row).astype(F32), tri.astype(BF16), _head_ones().astype(BF16))

    def chunk(c, carry):
        rows = pl.ds(pl.multiple_of(c * C, C), C)
        def pair(j):
            lanes = slice(j * LANES, (j + 1) * LANES)
            tok = (ref[rows, lanes] for ref in (r_ref, lw_ref, k_ref, v_ref, kk_ref, b_ref, g_ref))
            par = (ref[:, lanes] for ref in (rk_ref, lng_ref, lnb_ref))
            return _rwkv_chunk_pair(*tok, *par, st_ref[j], consts)

        stages = [pair(j) for j in range(pairs)]
        results = [None] * pairs
        while results[-1] is None:
            for j, gen in enumerate(stages):
                results[j] = next(gen)
        for j, (o, st_new) in enumerate(results):
            o_ref[rows, j * LANES:(j + 1) * LANES] = o
            st_ref[j] = st_new
        return carry

    lax.fori_loop(0, n_chunks, chunk, 0)

    @pl.when(pl.program_id(1) == pl.num_programs(1) - 1)
    def _():
        for j in range(pairs):
            st = st_ref[j]
            s_out_ref[0, 2 * j] = st[:HEAD_DIM, :HEAD_DIM]
            s_out_ref[0, 2 * j + 1] = st[HEAD_DIM:, HEAD_DIM:]


def _rwkv_scan(r, lw, k, v, kk, b, g, r_k, lnx_g, lnx_b, *, batch, seq, tb):
    nt = seq // tb
    pairs = RWKV_WIDTH // LANES
    tok = pl.BlockSpec((tb, RWKV_WIDTH), lambda bi, i: (bi * nt + i, 0))
    par = pl.BlockSpec((1, RWKV_WIDTH), lambda bi, i: (0, 0))
    return pl.pallas_call(
        _rwkv_scan_kernel,
        grid=(batch, nt),
        in_specs=[tok] * 7 + [par] * 3,
        out_specs=[tok, pl.BlockSpec((1, N_HEADS, HEAD_DIM, HEAD_DIM), lambda bi, i: (bi, 0, 0, 0))],
        out_shape=[jax.ShapeDtypeStruct((batch * seq, RWKV_WIDTH), F32),
                   jax.ShapeDtypeStruct((batch, N_HEADS, HEAD_DIM, HEAD_DIM), F32)],
        scratch_shapes=[pltpu.VMEM((pairs, LANES, LANES), F32)],
        compiler_params=_params("parallel", "arbitrary"),
        name="rwkv_scan",
    )(r, lw, k, v, kk, b, g, r_k, lnx_g, lnx_b)


def _rwkv_step_kernel(s_ref, r_ref, lw_ref, k_ref, v_ref, kk_ref, b_ref, g_ref, rk_ref, lng_ref, lnb_ref,
                      o_ref, s_out_ref):
    s = s_ref[...]
    r, k, v, kk, b, g = (ref[...] for ref in (r_ref, k_ref, v_ref, kk_ref, b_ref, g_ref))
    w = jnp.exp(lw_ref[...])
    shape = (HEAD_DIM, HEAD_DIM)
    eye = lax.broadcasted_iota(jnp.int32, shape, 0) == lax.broadcasted_iota(jnp.int32, shape, 1)
    sa = jnp.sum(s * (-kk), axis=-1, keepdims=True)
    v_col = jnp.sum(jnp.where(eye, v, 0.0), axis=-1, keepdims=True)
    s_new = s * w + sa * b + v_col * k
    s_out_ref[...] = s_new
    o_col = jnp.sum(s_new * r, axis=-1, keepdims=True)
    o = jnp.sum(jnp.where(eye, o_col, 0.0), axis=-2, keepdims=True)
    mean = jnp.mean(o, axis=-1, keepdims=True)
    d = o - mean
    var = jnp.mean(d * d, axis=-1, keepdims=True)
    on = d * lax.rsqrt(var + GN_EPS) * lng_ref[...] + lnb_ref[...]
    bonus = jnp.sum(r * k * rk_ref[...], axis=-1, keepdims=True) * v
    o_ref[...] = (on + bonus) * g


def _rwkv_step(state, r, lw, k, v, kk, b, g, r_k, lnx_g, lnx_b, *, bt):
    B = state.shape[0]
    vec4 = lambda a: a.reshape(B, N_HEADS, 1, HEAD_DIM)
    par4 = lambda a: a.reshape(1, N_HEADS, 1, HEAD_DIM)
    st_spec = pl.BlockSpec((bt, N_HEADS, HEAD_DIM, HEAD_DIM), lambda i: (i, 0, 0, 0))
    vec_spec = pl.BlockSpec((bt, N_HEADS, 1, HEAD_DIM), lambda i: (i, 0, 0, 0))
    par_spec = pl.BlockSpec((1, N_HEADS, 1, HEAD_DIM), lambda i: (0, 0, 0, 0))
    o, s_new = pl.pallas_call(
        _rwkv_step_kernel,
        grid=(B // bt,),
        in_specs=[st_spec] + [vec_spec] * 7 + [par_spec] * 3,
        out_specs=[vec_spec, st_spec],
        out_shape=[jax.ShapeDtypeStruct((B, N_HEADS, 1, HEAD_DIM), F32),
                   jax.ShapeDtypeStruct(state.shape, F32)],
        compiler_params=_params("parallel"),
        name="rwkv_step",
    )(state, *(vec4(a) for a in (r, lw, k, v, kk, b, g)), par4(r_k), par4(lnx_g), par4(lnx_b))
    return o.reshape(B, RWKV_WIDTH), s_new


def _swa_prompt_kernel(q_ref, kc_ref, kp_ref, vc_ref, vp_ref, slope_ref, sink_ref, o_ref):
    blk = q_ref.shape[0]
    has_prev = pl.program_id(2) > 0
    qi = lax.broadcasted_iota(jnp.int32, (blk, 2 * blk), 0) + blk
    kj = lax.broadcasted_iota(jnp.int32, (blk, 2 * blk), 1)
    dist = qi - kj
    valid = (dist >= 0) & (dist < WINDOW) & (has_prev | (kj >= blk))
    distf = dist.astype(F32)
    kv = []
    for kvh in range(2):
        lanes = slice(kvh * HEAD_DIM, (kvh + 1) * HEAD_DIM)
        kv.append((jnp.concatenate([kp_ref[:, lanes], kc_ref[:, lanes]], axis=0).astype(BF16),
                   jnp.concatenate([vp_ref[:, lanes], vc_ref[:, lanes]], axis=0).astype(BF16)))

    def head(h):
        keys, vals = kv[h // GQA_GROUP]
        cols = slice(h * HEAD_DIM, (h + 1) * HEAD_DIM)
        s = _dot_nt(q_ref[:, cols] * (HEAD_DIM ** -0.5), keys)
        yield
        s = jnp.where(valid, s - slope_ref[h] * distf, NEG_INF)
        sink = sink_ref[h][:, :1]
        m = jnp.maximum(jnp.max(s, axis=-1, keepdims=True), sink)
        e = jnp.exp(s - m)
        denom = jnp.sum(e, axis=-1, keepdims=True) + jnp.exp(sink - m)
        o = _dot(e, vals)
        yield
        o_ref[:, cols] = o / denom
        yield

    heads = [head(h) for h in range(2 * GQA_GROUP)]
    for _ in range(3):
        for gen in heads:
            next(gen)


def _swa_prompt(p_all, slopes, sinks, *, batch, seq):
    blk = WINDOW
    nb = seq // blk
    qw = 2 * GQA_GROUP * HEAD_DIM
    cur = lambda col: (lambda b, j, i: (b * nb + i, col + j))
    prev = lambda col: (lambda b, j, i: (b * nb + jnp.maximum(i - 1, 0), col + j))
    par = pl.BlockSpec((2 * GQA_GROUP, 1, 2 * blk), lambda b, j, i: (j, 0, 0))
    return pl.pallas_call(
        _swa_prompt_kernel,
        grid=(batch, N_KV_HEADS // 2, nb),
        in_specs=[pl.BlockSpec((blk, qw), cur(COL_Q // qw)),
                  pl.BlockSpec((blk, LANES), cur(COL_AK // LANES)),
                  pl.BlockSpec((blk, LANES), prev(COL_AK // LANES)),
                  pl.BlockSpec((blk, LANES), cur(COL_AV // LANES)),
                  pl.BlockSpec((blk, LANES), prev(COL_AV // LANES)),
                  par, par],
        out_specs=pl.BlockSpec((blk, qw), lambda b, j, i: (b * nb + i, j)),
        out_shape=jax.ShapeDtypeStruct((batch * seq, ATTN_WIDTH), F32),
        compiler_params=_params("parallel", "parallel", "arbitrary"),
        name="swa_prompt",
    )(p_all, p_all, p_all, p_all, p_all,
      jnp.broadcast_to(slopes.reshape(N_HEADS, 1, 1), (N_HEADS, 1, 2 * blk)),
      jnp.broadcast_to(sinks.reshape(N_HEADS, 1, 1), (N_HEADS, 1, 2 * blk)))


def _swa_sample_kernel(q_ref, kn_ref, vn_ref, ck_ref, cv_ref, slope_ref, sink_ref, o_ref, kw_ref, vw_ref):
    bt = q_ref.shape[0]
    win = ck_ref.shape[1]
    row = lax.broadcasted_iota(jnp.int32, (win, KV_WIDTH), 0)
    head_kv = lax.broadcasted_iota(jnp.int32, (N_HEADS, KV_WIDTH), 0) // GQA_GROUP
    lane_kv = lax.broadcasted_iota(jnp.int32, (N_HEADS, KV_WIDTH), 1) // HEAD_DIM
    own = head_kv == lane_kv
    dist = (win - 1 - lax.broadcasted_iota(jnp.int32, (N_HEADS, win), 1)).astype(F32)
    bias = slope_ref[...] * dist
    sink = sink_ref[...][:, :1]
    for i in range(bt):
        keys = jnp.where(row == win - 1, kn_ref[i:i + 1, :], pltpu.roll(ck_ref[i], win - 1, axis=0))
        vals = jnp.where(row == win - 1, vn_ref[i:i + 1, :], pltpu.roll(cv_ref[i], win - 1, axis=0))
        kw_ref[i] = keys
        vw_ref[i] = vals
        q = q_ref[i]
        qbd = jnp.where(own, jnp.concatenate([q] * N_KV_HEADS, axis=-1), 0.0)
        s = _dot_nt(qbd, keys) * (HEAD_DIM ** -0.5) - bias
        m = jnp.maximum(jnp.max(s, axis=-1, keepdims=True), sink)
        e = jnp.exp(s - m)
        denom = jnp.sum(e, axis=-1, keepdims=True) + jnp.exp(sink - m)
        o2 = jnp.where(own, _dot(e, vals), 0.0)
        o = o2[:, :HEAD_DIM]
        for c in range(1, N_KV_HEADS):
            o = o + o2[:, c * HEAD_DIM:(c + 1) * HEAD_DIM]
        o_ref[i] = o / denom


def _swa_sample(q, k_new, v_new, cache_k, cache_v, slopes, sinks, *, bt):
    B, win = cache_k.shape[0], cache_k.shape[1]
    q3 = pl.BlockSpec((bt, N_HEADS, HEAD_DIM), lambda i: (i, 0, 0))
    new = pl.BlockSpec((bt, KV_WIDTH), lambda i: (i, 0))
    cache = pl.BlockSpec((bt, win, KV_WIDTH), lambda i: (i, 0, 0))
    par = pl.BlockSpec((N_HEADS, win), lambda i: (0, 0))
    o, kw, vw = pl.pallas_call(
        _swa_sample_kernel,
        grid=(B // bt,),
        in_specs=[q3, new, new, cache, cache, par, par],
        out_specs=[q3, cache, cache],
        out_shape=[jax.ShapeDtypeStruct((B, N_HEADS, HEAD_DIM), F32),
                   jax.ShapeDtypeStruct((B, win, KV_WIDTH), F32),
                   jax.ShapeDtypeStruct((B, win, KV_WIDTH), F32)],
        compiler_params=_params("parallel"),
        name="swa_sample",
    )(q.reshape(B, N_HEADS, HEAD_DIM), k_new, v_new,
      cache_k.reshape(B, win, KV_WIDTH), cache_v.reshape(B, win, KV_WIDTH),
      jnp.broadcast_to(slopes.reshape(N_HEADS, 1), (N_HEADS, win)),
      jnp.broadcast_to(sinks.reshape(N_HEADS, 1), (N_HEADS, win)))
    return o.reshape(B, ATTN_WIDTH), kw, vw


def _permute_in_cols(a):
    attn_proj = ATTN_WIDTH + 2 * KV_WIDTH
    rkv = a[..., attn_proj:attn_proj + 3 * RWKV_WIDTH]
    lora = a[..., attn_proj + 3 * RWKV_WIDTH:]
    pad = jnp.zeros(a.shape[:-1] + (LORA_PAD - LORA_WIDTH,), a.dtype)
    return jnp.concatenate([rkv, a[..., :attn_proj], lora, pad], axis=-1)


def _lora_rows(w, start):
    return jnp.zeros((LORA_PAD, w.shape[1]), F32).at[start:start + w.shape[0]].set(w).astype(BF16)


def _pad_ff(w, axis, to):
    shape = list(w.shape)
    shape[axis] = to - w.shape[axis]
    return jnp.concatenate([w.astype(BF16), jnp.zeros(shape, BF16)], axis=axis)


def kernel(x_prompt, x_sample, state_rwkv_shift, state_rwkv_wkv, cache_swa_k, cache_swa_v, ffn1_norm, ffn1_w_gate, ffn1_w_up, ffn1_w_down, mix_norm, w_in, rwkv_mu, rwkv_decay_up, rwkv_decay_base, rwkv_iclr_up, rwkv_iclr_base, rwkv_gate_up, rwkv_k_k, rwkv_k_a, rwkv_r_k, rwkv_lnx_g, rwkv_lnx_b, attn_sinks, w_out, ffn2_norm, ffn2_w_gate, ffn2_w_up, ffn2_w_down, final_norm):
    batch, seq, d_model = x_prompt.shape
    dec_batch = x_sample.shape[0]
    depth = ffn1_norm.shape[0]
    n_prompt = batch * seq
    d_ff = ffn1_w_gate.shape[-1]
    ff_pad = -(-d_ff // 512) * 512
    win = cache_swa_k.shape[2]
    assert x_sample.shape[1] == 1 and seq % 512 == 0 and win == WINDOW and n_prompt % dec_batch == 0
    tm = 640
    assert (n_prompt + dec_batch) % tm == 0

    x = jnp.concatenate([x_prompt.reshape(n_prompt, d_model), x_sample.reshape(dec_batch, d_model)], axis=0)
    slopes = jnp.exp2(-8.0 * jnp.arange(1, N_HEADS + 1, dtype=F32) / N_HEADS)
    outs = [[] for _ in range(8)]
    for l in range(depth):
        x = _ffn(x, ffn1_norm[l], _pad_ff(ffn1_w_gate[l], 1, ff_pad), _pad_ff(ffn1_w_up[l], 1, ff_pad),
                 _pad_ff(ffn1_w_down[l], 0, ff_pad), tm=tm, tf=512)
        p_all = _inproj(x, mix_norm[l], _permute_in_cols(w_in[l]).astype(BF16), tm=tm, tn=IN_PAD // 3)

        mu = _permute_in_cols(jnp.concatenate([jnp.zeros((ATTN_WIDTH + 2 * KV_WIDTH,), F32), rwkv_mu[l]]))
        row = lambda a: a.reshape(1, -1)
        prep_w = (row(mu[COL_R:COL_K]), row(mu[COL_K:COL_V]), row(mu[COL_V:COL_Q]), row(mu[COL_LORA:]),
                  _lora_rows(rwkv_decay_up[l], 0), row(rwkv_decay_base[l]),
                  _lora_rows(rwkv_iclr_up[l], 64), row(rwkv_iclr_base[l]),
                  _lora_rows(rwkv_gate_up[l], 128), row(rwkv_k_k[l]), row(rwkv_k_a[l]))
        head_par = (row(rwkv_r_k[l]), row(rwkv_lnx_g[l]), row(rwkv_lnx_b[l]))

        prep = _prep_prompt(p_all, prep_w, batch=batch, seq=seq, tp=256)
        o_rwkv_p, wkv_p = _rwkv_scan(*prep, *head_par, batch=batch, seq=seq, tb=256)
        o_attn_p = _swa_prompt(p_all, slopes, attn_sinks[l], batch=batch, seq=seq)

        sh = _permute_in_cols(jnp.concatenate(
            [jnp.zeros((dec_batch, ATTN_WIDTH + 2 * KV_WIDTH), F32), state_rwkv_shift[l]], axis=-1))
        prep_s = _prep_sample(p_all, sh[:, COL_R:COL_K], sh[:, COL_K:COL_V], sh[:, COL_V:COL_Q], sh[:, COL_LORA:],
                              prep_w, row0=n_prompt, rows=dec_batch)
        o_rwkv_s, wkv_s = _rwkv_step(state_rwkv_wkv[l], *prep_s, *head_par, bt=8)
        p_s = p_all[n_prompt:]
        o_attn_s, kwin_s, vwin_s = _swa_sample(p_s[:, COL_Q:COL_AK], p_s[:, COL_AK:COL_AV], p_s[:, COL_AV:COL_LORA],
                                               cache_swa_k[l], cache_swa_v[l], slopes, attn_sinks[l], bt=8)

        x = _outproj(x, jnp.concatenate([o_rwkv_p, o_rwkv_s], axis=0), jnp.concatenate([o_attn_p, o_attn_s], axis=0),
                     w_out[l].astype(BF16), tm=tm)
        x = _ffn(x, ffn2_norm[l], _pad_ff(ffn2_w_gate[l], 1, ff_pad), _pad_ff(ffn2_w_up[l], 1, ff_pad),
                 _pad_ff(ffn2_w_down[l], 0, ff_pad), final_norm if l == depth - 1 else None, tm=tm, tf=512)

        wb = min(WINDOW, seq)
        tails = [p_all[(b + 1) * seq - wb:(b + 1) * seq] for b in range(batch)]
        last = jnp.concatenate([t[wb - 1:] for t in tails], axis=0)
        shift_cols = lambda a: jnp.concatenate([a[:, COL_R:COL_Q], a[:, COL_LORA:COL_LORA + LORA_WIDTH]], axis=-1)
        for lst, val in zip(outs, (
                shift_cols(last), wkv_p,
                jnp.stack([t[:, COL_AK:COL_AV] for t in tails]).reshape(batch, wb, N_KV_HEADS, HEAD_DIM),
                jnp.stack([t[:, COL_AV:COL_LORA] for t in tails]).reshape(batch, wb, N_KV_HEADS, HEAD_DIM),
                shift_cols(p_s), wkv_s,
                kwin_s.reshape(dec_batch, win, N_KV_HEADS, HEAD_DIM),
                vwin_s.reshape(dec_batch, win, N_KV_HEADS, HEAD_DIM))):
            lst.append(val)

    y_prompt = x[:n_prompt].reshape(batch, seq, d_model)
    y_sample = x[n_prompt:].reshape(dec_batch, 1, d_model)
    return (y_prompt, y_sample) + tuple(jnp.stack(o) for o in outs)
```

```python
import functools

import jax
import jax.numpy as jnp
from jax import lax
from jax.experimental import pallas as pl
from jax.experimental.pallas import tpu as pltpu

F32 = jnp.float32
BF16 = jnp.bfloat16
HIGHEST = lax.Precision.HIGHEST

HEAD_DIM = 64
N_HEADS = 16
N_KV_HEADS = 4
GQA_GROUP = 4
RWKV_WIDTH = 1024
ATTN_WIDTH = 1024
KV_WIDTH = 256
WINDOW = 128
LORA_WIDTH = 64 + 64 + 160
LORA_PAD = 384
RMS_EPS = 1e-6
GN_EPS = 64e-5
NEG_INF = -1e30

COL_R, COL_K, COL_V = 0, 1024, 2048
COL_Q = 3072
COL_AK = 4096
COL_AV = 4352
COL_LORA = 4608
IN_PAD = COL_LORA + LORA_PAD

LANES = 128
CHUNK = 64
VMEM_LIMIT = 56 * 1024 * 1024


def _dot(a, b):
    return jnp.dot(a.astype(BF16), b.astype(BF16), preferred_element_type=F32)


def _dot_nt(a, b):
    return lax.dot_general(a.astype(BF16), b.astype(BF16), (((1,), (1,)), ((), ())),
                           preferred_element_type=F32)


def _dot_hi(a, b):
    return jnp.dot(a, b, preferred_element_type=F32, precision=HIGHEST)


def _rms(x, g):
    return x * lax.rsqrt(jnp.mean(x * x, axis=-1, keepdims=True) + RMS_EPS) * g


def _params(*sem):
    return pltpu.CompilerParams(dimension_semantics=sem, vmem_limit_bytes=VMEM_LIMIT)


def _ffn_kernel(*refs, final, d_ff, tail_in, tail_out):
    refs = list(refs)
    x_ref = refs.pop(0)
    xt_ref = refs.pop(0) if tail_in else None
    g_ref, wg_ref, wu_ref, wd_ref = (refs.pop(0) for _ in range(4))
    fg_ref = refs.pop(0) if final else None
    o_ref = refs.pop(0)
    ot_ref = refs.pop(0) if tail_out else None
    (xn_ref,) = refs
    f = pl.program_id(1)
    last_tile = pl.program_id(0) == pl.num_programs(0) - 1
    tm = x_ref.shape[0]
    tf = wg_ref.shape[1]
    nf = -(-d_ff // tf)
    last_width = d_ff - (nf - 1) * tf

    def start(x):
        xn_ref[...] = _rms(x, g_ref[...]).astype(BF16)
        o_ref[...] = x

    if tail_in:
        own = tm - xt_ref.shape[0]
        pl.when((f == 0) & jnp.logical_not(last_tile))(lambda: start(x_ref[...]))
        pl.when((f == 0) & last_tile)(lambda: start(jnp.concatenate([x_ref[:own, :], xt_ref[...]], axis=0)))
    else:
        pl.when(f == 0)(lambda: start(x_ref[...]))

    def accumulate(width):
        xn = xn_ref[...]
        hg = jnp.dot(xn, wg_ref[:, :width].astype(BF16), preferred_element_type=F32)
        hu = jnp.dot(xn, wu_ref[:, :width].astype(BF16), preferred_element_type=F32)
        h = (hg * jax.nn.sigmoid(hg) * (0.5 * hu)).astype(BF16)
        o_ref[...] += jnp.dot(h, wd_ref[:width, :].astype(BF16), preferred_element_type=F32)

    if last_width == tf:
        accumulate(tf)
    else:
        pl.when(f < nf - 1)(lambda: accumulate(tf))
        pl.when(f == nf - 1)(lambda: accumulate(last_width))

    if final:
        @pl.when(f == nf - 1)
        def _():
            o_ref[...] = _rms(o_ref[...], fg_ref[...])

    if tail_out:
        @pl.when((f == nf - 1) & last_tile)
        def _():
            ot_ref[...] = o_ref[tm - ot_ref.shape[0]:, :]


def _ffn(x, norm_g, wg, wu, wd, final_g=None, *, tm, tf, x_tail=None, split_tail=0):
    D = x.shape[1]
    M = x.shape[0] + (0 if x_tail is None else x_tail.shape[0])
    d_ff = wg.shape[1]
    final = final_g is not None
    const = lambda i, f: (0, 0)
    in_specs = [pl.BlockSpec((tm, D), lambda i, f: (i, 0), pipeline_mode=pl.Buffered(1))]
    args = [x]
    if x_tail is not None:
        in_specs.append(pl.BlockSpec(x_tail.shape, const))
        args.append(x_tail)
    in_specs += [pl.BlockSpec((1, D), const),
                 pl.BlockSpec((D, tf), lambda i, f: (0, f)),
                 pl.BlockSpec((D, tf), lambda i, f: (0, f)),
                 pl.BlockSpec((tf, D), lambda i, f: (f, 0))]
    args += [norm_g.reshape(1, D), wg, wu, wd]
    if final:
        in_specs.append(pl.BlockSpec((1, D), const))
        args.append(final_g.reshape(1, D))
    out_specs = [pl.BlockSpec((tm, D), lambda i, f: (i, 0))]
    out_shape = [jax.ShapeDtypeStruct((M - split_tail, D), F32)]
    if split_tail:
        out_specs.append(pl.BlockSpec((split_tail, D), const))
        out_shape.append(jax.ShapeDtypeStruct((split_tail, D), F32))
    out = pl.pallas_call(
        functools.partial(_ffn_kernel, final=final, d_ff=d_ff, tail_in=x_tail is not None,
                          tail_out=bool(split_tail)),
        grid=(M // tm, pl.cdiv(d_ff, tf)),
        in_specs=in_specs,
        out_specs=out_specs,
        out_shape=out_shape,
        scratch_shapes=[pltpu.VMEM((tm, D), BF16)],
        compiler_params=_params("arbitrary", "arbitrary"),
        name="ffn",
    )(*args)
    return out if split_tail else out[0]


def _inproj_kernel(x_ref, g_ref, w_ref, o_ref, xn_ref):
    @pl.when(pl.program_id(1) == 0)
    def _():
        xn_ref[...] = _rms(x_ref[...], g_ref[...]).astype(BF16)

    o_ref[...] = jnp.dot(xn_ref[...], w_ref[...], preferred_element_type=F32)


def _inproj(x, norm_g, w, *, tm, tn):
    M, D = x.shape
    N = w.shape[1]
    return pl.pallas_call(
        _inproj_kernel,
        grid=(M // tm, N // tn),
        in_specs=[pl.BlockSpec((tm, D), lambda i, j: (i, 0)),
                  pl.BlockSpec((1, D), lambda i, j: (0, 0)),
                  pl.BlockSpec((D, tn), lambda i, j: (0, j))],
        out_specs=pl.BlockSpec((tm, tn), lambda i, j: (i, j)),
        out_shape=jax.ShapeDtypeStruct((M, N), F32),
        scratch_shapes=[pltpu.VMEM((tm, D), BF16)],
        compiler_params=_params("parallel", "arbitrary"),
        name="inproj",
    )(x, norm_g.reshape(1, D), w)


def _outproj_kernel(x_ref, oa_ref, ob_ref, w_ref, o_ref):
    half = oa_ref.shape[1]
    o_ref[...] = (x_ref[...]
                  + jnp.dot(oa_ref[...].astype(BF16), w_ref[:half, :], preferred_element_type=F32)
                  + jnp.dot(ob_ref[...].astype(BF16), w_ref[half:, :], preferred_element_type=F32))


def _outproj(x, o_rwkv, o_attn, w, *, tm):
    M, D = x.shape
    Wd = o_rwkv.shape[1]
    return pl.pallas_call(
        _outproj_kernel,
        grid=(M // tm,),
        in_specs=[pl.BlockSpec((tm, D), lambda i: (i, 0)),
                  pl.BlockSpec((tm, Wd), lambda i: (i, 0)),
                  pl.BlockSpec((tm, Wd), lambda i: (i, 0)),
                  pl.BlockSpec(w.shape, lambda i: (0, 0))],
        out_specs=pl.BlockSpec((tm, D), lambda i: (i, 0)),
        out_shape=jax.ShapeDtypeStruct((M, D), F32),
        compiler_params=_params("parallel"),
        name="outproj",
    )(x, o_rwkv, o_attn, w)


def _head_ones():
    r = lax.broadcasted_iota(jnp.int32, (LANES, LANES), 0) // HEAD_DIM
    c = lax.broadcasted_iota(jnp.int32, (LANES, LANES), 1) // HEAD_DIM
    return (r == c).astype(F32)


def _head_sum(x):
    ones = _head_ones().astype(BF16)
    terms = _split_terms(x, 2)
    parts = [sum(jnp.dot(t[:, j * LANES:(j + 1) * LANES], ones, preferred_element_type=F32) for t in terms)
             for j in range(x.shape[1] // LANES)]
    return parts[0] if len(parts) == 1 else jnp.concatenate(parts, axis=-1)


def _rwkv_prep_math(p_r, p_k, p_v, p_l, q_r, q_k, q_v, q_l, w):
    (mu_r, mu_k, mu_v, mu_l, dec_up, dec_base, icl_up, icl_base, gate_up, k_k, k_a) = w
    r = p_r + mu_r * (q_r - p_r)
    k = p_k + mu_k * (q_k - p_k)
    v = p_v + mu_v * (q_v - p_v)
    ul = p_l + mu_l * (q_l - p_l)
    w_log = -jax.nn.softplus(-(dec_base + _dot(jnp.tanh(ul), dec_up))) - 0.5
    lw = -jnp.exp(w_log)
    a = jax.nn.sigmoid(icl_base + _dot(ul, icl_up))
    g = _dot(jax.nn.sigmoid(ul), gate_up)
    kk = k * k_k
    kk = kk / jnp.maximum(jnp.sqrt(_head_sum(kk * kk)), 1e-12)
    k2 = k * (1.0 + (a - 1.0) * k_a)
    return r, lw, k2, v, kk, kk * a, g


def _prep_sample_kernel(pr, pk, pv, pL, qr, qk, qv, qL, *rest):
    w_refs, out_refs = rest[:11], rest[11:]
    outs = _rwkv_prep_math(pr[...], pk[...], pv[...], pL[...], qr[...], qk[...], qv[...], qL[...],
                           tuple(r[...] for r in w_refs))
    for o_ref, o in zip(out_refs, outs):
        o_ref[...] = o


def _prep_weight_specs(w, nargs):
    zero = (lambda b, i: (0, 0)) if nargs == 2 else (lambda i: (0, 0))
    return [pl.BlockSpec(a.shape, zero) for a in w]


def _prep_sample(p_all, shift_r, shift_k, shift_v, shift_l, w, *, row0, rows):
    W = RWKV_WIDTH
    rb = row0 // rows

    def cur(col_block, width):
        return pl.BlockSpec((rows, width), lambda i: (rb, col_block))

    def full(width):
        return pl.BlockSpec((rows, width), lambda i: (0, 0))

    in_specs = [cur(0, W), cur(1, W), cur(2, W), cur(COL_LORA // LORA_PAD, LORA_PAD),
                full(W), full(W), full(W), full(LORA_PAD)]
    in_specs += _prep_weight_specs(w, 1)
    out = jax.ShapeDtypeStruct((rows, W), F32)
    return pl.pallas_call(
        _prep_sample_kernel,
        grid=(1,),
        in_specs=in_specs,
        out_specs=[pl.BlockSpec((rows, W), lambda i: (0, 0))] * 7,
        out_shape=[out] * 7,
        compiler_params=_params("arbitrary"),
        name="rwkv_prep_sample",
    )(p_all, p_all, p_all, p_all, shift_r, shift_k, shift_v, shift_l, *w)


def _stack_heads(x):
    lane = lax.broadcasted_iota(jnp.int32, x.shape, 1)
    return jnp.concatenate([jnp.where(lane < HEAD_DIM, x, 0.0), jnp.where(lane >= HEAD_DIM, x, 0.0)], axis=0)


def _split_terms(x, n):
    terms = []
    for _ in range(n):
        hi = x.astype(BF16)
        terms.append(hi)
        x = x - hi.astype(F32)
    return terms


def _rwkv_chunk_pair(r, lw, k, v, kk, b, g, rk, lng, lnb, st, consts):
    C = CHUNK
    strict, incl, eye, tri, ones = consts
    cum = sum(jnp.dot(tri, t, preferred_element_type=F32) for t in _split_terms(lw, 3))
    yield
    total = cum[C - 1:C, :]
    rg = _stack_heads(r * jnp.exp(cum))
    nag = _stack_heads(-kk * jnp.exp(cum - lw))
    ginv = jnp.exp(-cum)
    bi = _stack_heads(b * ginv)
    ki = _stack_heads(k * ginv)
    gend = jnp.exp(total - cum)
    bend = _stack_heads(b * gend)
    kend = _stack_heads(k * gend)
    vs = _stack_heads(v)

    gram = _dot_nt(jnp.concatenate([nag, rg], axis=0), jnp.concatenate([bi, ki], axis=0))
    yield
    a_ab = jnp.where(strict, gram[:2 * C, :2 * C], 0.0)
    a_ak = jnp.where(strict, gram[:2 * C, 2 * C:], 0.0)
    a_rb = jnp.where(incl, gram[2 * C:, :2 * C], 0.0)
    a_rk = jnp.where(incl, gram[2 * C:, 2 * C:], 0.0)

    def head_sum(x):
        return sum(jnp.dot(t, ones, preferred_element_type=F32) for t in _split_terms(x, 2))

    z = _dot_nt(nag, st) + _dot(a_ak, vs)
    o2 = _dot_nt(rg, st) + _dot(a_rk, vs)
    bonus = head_sum(r * k * rk) * v
    yield

    pw = a_ab
    inv = eye + pw
    span = 1
    while span * 2 < C:
        pw = _dot(pw, pw)
        yield
        inv = inv + _dot(inv, pw)
        span *= 2
    yield

    u = _dot(inv, z)
    yield
    o2 = o2 + _dot(a_rb, u)
    st_new = st * jnp.exp(total) + _dot(u.T, bend) + _dot(vs.T, kend)
    yield
    o = o2[:C, :] + o2[C:, :]
    mean = head_sum(o) * (1.0 / HEAD_DIM)
    yield
    d = o - mean
    var = head_sum(d * d) * (1.0 / HEAD_DIM)
    yield
    on = d * lax.rsqrt(var + GN_EPS) * lng + lnb
    yield (on + bonus) * g, st_new


def _rwkv_scan_kernel(pr, pk, pv, pL, qr, qk, qv, qL, *rest):
    w_refs = rest[:11]
    rk_ref, lng_ref, lnb_ref, o_ref, s_out_ref = rest[11:16]
    tok_refs = rest[16:23]
    st_ref = rest[23]
    r_ref, lw_ref, k_ref, v_ref, kk_ref, b_ref, g_ref = tok_refs
    C = CHUNK
    n_chunks = pr.shape[0] // C
    pairs = pr.shape[1] // LANES
    first = pl.program_id(1) == 0

    @pl.when(first)
    def _():
        st_ref[...] = jnp.zeros_like(st_ref)

    def prev_rows(cur_ref, tail_ref):
        cur = cur_ref[...]
        tail = jnp.where(first, 0.0, tail_ref[7:8, :])
        row0 = lax.broadcasted_iota(jnp.int32, cur.shape, 0) == 0
        return jnp.where(row0, tail, pltpu.roll(cur, 1, axis=0))

    prepared = _rwkv_prep_math(pr[...], pk[...], pv[...], pL[...],
                               prev_rows(pr, qr), prev_rows(pk, qk), prev_rows(pv, qv), prev_rows(pL, qL),
                               tuple(r[...] for r in w_refs))
    for ref, val in zip(tok_refs, prepared):
        ref[...] = val

    row = lax.broadcasted_iota(jnp.int32, (2 * C, 2 * C), 0)
    col = lax.broadcasted_iota(jnp.int32, (2 * C, 2 * C), 1)
    tri = lax.broadcasted_iota(jnp.int32, (C, C), 1) <= lax.broadcasted_iota(jnp.int32, (C, C), 0)
    consts = (col < row, col <= row, (col == row).astype(F32), tri.astype(BF16), _head_ones().astype(BF16))

    def chunk(c, carry):
        rows = pl.ds(pl.multiple_of(c * C, C), C)
        def pair(j):
            lanes = slice(j * LANES, (j + 1) * LANES)
            tok = (ref[rows, lanes] for ref in (r_ref, lw_ref, k_ref, v_ref, kk_ref, b_ref, g_ref))
            par = (ref[:, lanes] for ref in (rk_ref, lng_ref, lnb_ref))
            return _rwkv_chunk_pair(*tok, *par, st_ref[j], consts)

        stages = [pair(j) for j in range(pairs)]
        results = [None] * pairs
        while results[-1] is None:
            for j, gen in enumerate(stages):
                results[j] = next(gen)
        for j, (o, st_new) in enumerate(results):
            o_ref[rows, j * LANES:(j + 1) * LANES] = o
            st_ref[j] = st_new
        return carry

    lax.fori_loop(0, n_chunks, chunk, 0)

    @pl.when(pl.program_id(1) == pl.num_programs(1) - 1)
    def _():
        for j in range(pairs):
            st = st_ref[j]
            s_out_ref[0, 2 * j] = st[:HEAD_DIM, :HEAD_DIM]
            s_out_ref[0, 2 * j + 1] = st[HEAD_DIM:, HEAD_DIM:]


def _rwkv_scan(p_all, prep_w, r_k, lnx_g, lnx_b, *, batch, seq, tb):
    nt = seq // tb
    W = RWKV_WIDTH
    pairs = W // LANES

    def cur(col_block, width):
        return pl.BlockSpec((tb, width), lambda b, i: (b * nt + i, col_block))

    def tail(col_block, width):
        return pl.BlockSpec((8, width), lambda b, i: (jnp.maximum((b * seq + i * tb) // 8 - 1, 0), col_block))

    lora_block = COL_LORA // LORA_PAD
    in_specs = [cur(0, W), cur(1, W), cur(2, W), cur(lora_block, LORA_PAD),
                tail(0, W), tail(1, W), tail(2, W), tail(lora_block, LORA_PAD)]
    in_specs += _prep_weight_specs(prep_w, 2)
    in_specs += [pl.BlockSpec((1, W), lambda b, i: (0, 0))] * 3
    return pl.pallas_call(
        _rwkv_scan_kernel,
        grid=(batch, nt),
        in_specs=in_specs,
        out_specs=[pl.BlockSpec((tb, W), lambda b, i: (b * nt + i, 0)),
                   pl.BlockSpec((1, N_HEADS, HEAD_DIM, HEAD_DIM), lambda b, i: (b, 0, 0, 0))],
        out_shape=[jax.ShapeDtypeStruct((batch * seq, W), F32),
                   jax.ShapeDtypeStruct((batch, N_HEADS, HEAD_DIM, HEAD_DIM), F32)],
        scratch_shapes=[pltpu.VMEM((tb, W), F32)] * 7 + [pltpu.VMEM((pairs, LANES, LANES), F32)],
        compiler_params=_params("parallel", "arbitrary"),
        name="rwkv_scan",
    )(*([p_all] * 8), *prep_w, r_k, lnx_g, lnx_b)


def _rwkv_step_kernel(s_ref, r_ref, lw_ref, k_ref, v_ref, kk_ref, b_ref, g_ref, rk_ref, lng_ref, lnb_ref,
                      o_ref, s_out_ref):
    s = s_ref[...]
    r, k, v, kk, b, g = (ref[...] for ref in (r_ref, k_ref, v_ref, kk_ref, b_ref, g_ref))
    w = jnp.exp(lw_ref[...])
    shape = (HEAD_DIM, HEAD_DIM)
    eye = lax.broadcasted_iota(jnp.int32, shape, 0) == lax.broadcasted_iota(jnp.int32, shape, 1)
    sa = jnp.sum(s * (-kk), axis=-1, keepdims=True)
    v_col = jnp.sum(jnp.where(eye, v, 0.0), axis=-1, keepdims=True)
    s_new = s * w + sa * b + v_col * k
    s_out_ref[...] = s_new
    o_col = jnp.sum(s_new * r, axis=-1, keepdims=True)
    o = jnp.sum(jnp.where(eye, o_col, 0.0), axis=-2, keepdims=True)
    mean = jnp.mean(o, axis=-1, keepdims=True)
    d = o - mean
    var = jnp.mean(d * d, axis=-1, keepdims=True)
    on = d * lax.rsqrt(var + GN_EPS) * lng_ref[...] + lnb_ref[...]
    bonus = jnp.sum(r * k * rk_ref[...], axis=-1, keepdims=True) * v
    o_ref[...] = (on + bonus) * g


def _rwkv_step(state, r, lw, k, v, kk, b, g, r_k, lnx_g, lnx_b, *, bt):
    B = state.shape[0]
    vec4 = lambda a: a.reshape(B, N_HEADS, 1, HEAD_DIM)
    par4 = lambda a: a.reshape(1, N_HEADS, 1, HEAD_DIM)
    st_spec = pl.BlockSpec((bt, N_HEADS, HEAD_DIM, HEAD_DIM), lambda i: (i, 0, 0, 0))
    vec_spec = pl.BlockSpec((bt, N_HEADS, 1, HEAD_DIM), lambda i: (i, 0, 0, 0))
    par_spec = pl.BlockSpec((1, N_HEADS, 1, HEAD_DIM), lambda i: (0, 0, 0, 0))
    o, s_new = pl.pallas_call(
        _rwkv_step_kernel,
        grid=(B // bt,),
        in_specs=[st_spec] + [vec_spec] * 7 + [par_spec] * 3,
        out_specs=[vec_spec, st_spec],
        out_shape=[jax.ShapeDtypeStruct((B, N_HEADS, 1, HEAD_DIM), F32),
                   jax.ShapeDtypeStruct(state.shape, F32)],
        compiler_params=_params("parallel"),
        name="rwkv_step",
    )(state, *(vec4(a) for a in (r, lw, k, v, kk, b, g)), par4(r_k), par4(lnx_g), par4(lnx_b))
    return o.reshape(B, RWKV_WIDTH), s_new


def _swa_prompt_kernel(q_ref, kc_ref, kp_ref, vc_ref, vp_ref, slope_ref, sink_ref, o_ref):
    blk = q_ref.shape[0]
    has_prev = pl.program_id(2) > 0
    kj = lax.broadcasted_iota(jnp.int32, (2 * blk, blk), 0)
    qi = lax.broadcasted_iota(jnp.int32, (2 * blk, blk), 1) + blk
    dist = qi - kj
    valid = (dist >= 0) & (dist < WINDOW) & (has_prev | (kj >= blk))
    distf = dist.astype(F32)

    kcat = jnp.concatenate([kp_ref[...], kc_ref[...]], axis=0)
    kroll = pltpu.roll(kcat, HEAD_DIM, axis=1)
    low_lane = lax.broadcasted_iota(jnp.int32, kcat.shape, 1) < HEAD_DIM
    vt = jnp.concatenate([vp_ref[...], vc_ref[...]], axis=0).T
    vroll = pltpu.roll(vt, HEAD_DIM, axis=0)
    low_row = lax.broadcasted_iota(jnp.int32, vt.shape, 0) < HEAD_DIM
    kv = [(jnp.where(low_lane, kcat, 0.0).astype(BF16), jnp.where(low_lane, 0.0, kroll).astype(BF16),
           jnp.where(low_row, vt, 0.0).astype(BF16), jnp.where(low_row, 0.0, vroll).astype(BF16)),
          (jnp.where(low_lane, kroll, 0.0).astype(BF16), jnp.where(low_lane, 0.0, kcat).astype(BF16),
           jnp.where(low_row, vroll, 0.0).astype(BF16), jnp.where(low_row, 0.0, vt).astype(BF16))]
    out_low_row = lax.broadcasted_iota(jnp.int32, (LANES, blk), 0) < HEAD_DIM

    def softmax_t(s, h):
        s = jnp.where(valid, s - slope_ref[h][:, :blk] * distf, NEG_INF)
        sink = sink_ref[h][:, :blk]
        m = jnp.maximum(jnp.max(s, axis=0, keepdims=True), sink)
        e = jnp.exp(s - m)
        return e, jnp.sum(e, axis=0, keepdims=True) + jnp.exp(sink - m)

    def head_pair(i):
        k_lo, k_hi, v_lo, v_hi = kv[i // 2]
        cols = slice(i * LANES, (i + 1) * LANES)
        q = (q_ref[:, cols] * (HEAD_DIM ** -0.5)).astype(BF16)
        s_lo = lax.dot_general(k_lo, q, (((1,), (1,)), ((), ())), preferred_element_type=F32)
        s_hi = lax.dot_general(k_hi, q, (((1,), (1,)), ((), ())), preferred_element_type=F32)
        yield
        e_lo, d_lo = softmax_t(s_lo, 2 * i)
        e_hi, d_hi = softmax_t(s_hi, 2 * i + 1)
        ot = (jnp.dot(v_lo, e_lo.astype(BF16), preferred_element_type=F32)
              + jnp.dot(v_hi, e_hi.astype(BF16), preferred_element_type=F32))
        yield
        o_ref[:, cols] = (ot / jnp.where(out_low_row, d_lo, d_hi)).T
        yield

    pairs = [head_pair(i) for i in range(q_ref.shape[1] // LANES)]
    for _ in range(3):
        for gen in pairs:
            next(gen)


def _swa_prompt(p_all, slopes, sinks, *, batch, seq):
    blk = WINDOW
    nb = seq // blk
    qw = 2 * GQA_GROUP * HEAD_DIM
    cur = lambda col: (lambda b, j, i: (b * nb + i, col + j))
    prev = lambda col: (lambda b, j, i: (b * nb + jnp.maximum(i - 1, 0), col + j))
    par = pl.BlockSpec((2 * GQA_GROUP, 1, 2 * blk), lambda b, j, i: (j, 0, 0))
    return pl.pallas_call(
        _swa_prompt_kernel,
        grid=(batch, N_KV_HEADS // 2, nb),
        in_specs=[pl.BlockSpec((blk, qw), cur(COL_Q // qw)),
                  pl.BlockSpec((blk, LANES), cur(COL_AK // LANES)),
                  pl.BlockSpec((blk, LANES), prev(COL_AK // LANES)),
                  pl.BlockSpec((blk, LANES), cur(COL_AV // LANES)),
                  pl.BlockSpec((blk, LANES), prev(COL_AV // LANES)),
                  par, par],
        out_specs=pl.BlockSpec((blk, qw), lambda b, j, i: (b * nb + i, j)),
        out_shape=jax.ShapeDtypeStruct((batch * seq, ATTN_WIDTH), F32),
        compiler_params=_params("parallel", "parallel", "arbitrary"),
        name="swa_prompt",
    )(p_all, p_all, p_all, p_all, p_all,
      jnp.broadcast_to(slopes.reshape(N_HEADS, 1, 1), (N_HEADS, 1, 2 * blk)),
      jnp.broadcast_to(sinks.reshape(N_HEADS, 1, 1), (N_HEADS, 1, 2 * blk)))


def _swa_sample_kernel(q_ref, kn_ref, vn_ref, ck_ref, cv_ref, slope_ref, sink_ref, o_ref, kw_ref, vw_ref):
    bt = q_ref.shape[0]
    win = ck_ref.shape[1]
    row = lax.broadcasted_iota(jnp.int32, (win, KV_WIDTH), 0)
    head_kv = lax.broadcasted_iota(jnp.int32, (N_HEADS, KV_WIDTH), 0) // GQA_GROUP
    lane_kv = lax.broadcasted_iota(jnp.int32, (N_HEADS, KV_WIDTH), 1) // HEAD_DIM
    own = head_kv == lane_kv
    dist = (win - 1 - lax.broadcasted_iota(jnp.int32, (N_HEADS, win), 1)).astype(F32)
    bias = slope_ref[...] * dist
    sink = sink_ref[...][:, :1]
    for i in range(bt):
        keys = jnp.where(row == win - 1, kn_ref[i:i + 1, :], pltpu.roll(ck_ref[i], win - 1, axis=0))
        vals = jnp.where(row == win - 1, vn_ref[i:i + 1, :], pltpu.roll(cv_ref[i], win - 1, axis=0))
        kw_ref[i] = keys
        vw_ref[i] = vals
        q = q_ref[i]
        qbd = jnp.where(own, jnp.concatenate([q] * N_KV_HEADS, axis=-1), 0.0)
        s = _dot_nt(qbd, keys) * (HEAD_DIM ** -0.5) - bias
        m = jnp.maximum(jnp.max(s, axis=-1, keepdims=True), sink)
        e = jnp.exp(s - m)
        denom = jnp.sum(e, axis=-1, keepdims=True) + jnp.exp(sink - m)
        o2 = jnp.where(own, _dot(e, vals), 0.0)
        o = o2[:, :HEAD_DIM]
        for c in range(1, N_KV_HEADS):
            o = o + o2[:, c * HEAD_DIM:(c + 1) * HEAD_DIM]
        o_ref[i] = o / denom


def _swa_sample(q, k_new, v_new, cache_k, cache_v, slopes, sinks, *, bt):
    B, win = cache_k.shape[0], cache_k.shape[1]
    q3 = pl.BlockSpec((bt, N_HEADS, HEAD_DIM), lambda i: (i, 0, 0))
    new = pl.BlockSpec((bt, KV_WIDTH), lambda i: (i, 0))
    cache = pl.BlockSpec((bt, win, KV_WIDTH), lambda i: (i, 0, 0))
    par = pl.BlockSpec((N_HEADS, win), lambda i: (0, 0))
    o, kw, vw = pl.pallas_call(
        _swa_sample_kernel,
        grid=(B // bt,),
        in_specs=[q3, new, new, cache, cache, par, par],
        out_specs=[q3, cache, cache],
        out_shape=[jax.ShapeDtypeStruct((B, N_HEADS, HEAD_DIM), F32),
                   jax.ShapeDtypeStruct((B, win, KV_WIDTH), F32),
                   jax.ShapeDtypeStruct((B, win, KV_WIDTH), F32)],
        compiler_params=_params("parallel"),
        name="swa_sample",
    )(q.reshape(B, N_HEADS, HEAD_DIM), k_new, v_new,
      cache_k.reshape(B, win, KV_WIDTH), cache_v.reshape(B, win, KV_WIDTH),
      jnp.broadcast_to(slopes.reshape(N_HEADS, 1), (N_HEADS, win)),
      jnp.broadcast_to(sinks.reshape(N_HEADS, 1), (N_HEADS, win)))
    return o.reshape(B, ATTN_WIDTH), kw, vw


def _permute_in_cols(a):
    attn_proj = ATTN_WIDTH + 2 * KV_WIDTH
    rkv = a[..., attn_proj:attn_proj + 3 * RWKV_WIDTH]
    lora = a[..., attn_proj + 3 * RWKV_WIDTH:]
    pad = jnp.zeros(a.shape[:-1] + (LORA_PAD - LORA_WIDTH,), a.dtype)
    return jnp.concatenate([rkv, a[..., :attn_proj], lora, pad], axis=-1)


def _lora_rows(w, start):
    return jnp.zeros((LORA_PAD, w.shape[1]), F32).at[start:start + w.shape[0]].set(w).astype(BF16)


def kernel(x_prompt, x_sample, state_rwkv_shift, state_rwkv_wkv, cache_swa_k, cache_swa_v, ffn1_norm, ffn1_w_gate, ffn1_w_up, ffn1_w_down, mix_norm, w_in, rwkv_mu, rwkv_decay_up, rwkv_decay_base, rwkv_iclr_up, rwkv_iclr_base, rwkv_gate_up, rwkv_k_k, rwkv_k_a, rwkv_r_k, rwkv_lnx_g, rwkv_lnx_b, attn_sinks, w_out, ffn2_norm, ffn2_w_gate, ffn2_w_up, ffn2_w_down, final_norm):
    batch, seq, d_model = x_prompt.shape
    dec_batch = x_sample.shape[0]
    depth = ffn1_norm.shape[0]
    n_prompt = batch * seq
    win = cache_swa_k.shape[2]
    assert x_sample.shape[1] == 1 and seq % 512 == 0 and win == WINDOW and n_prompt % dec_batch == 0
    tm, tm_ffn, tf = 640, 832, 512
    assert (n_prompt + dec_batch) % tm == 0 and (n_prompt + dec_batch) % tm_ffn == 0

    x = x_prompt.reshape(n_prompt, d_model)
    slopes = jnp.exp2(-8.0 * jnp.arange(1, N_HEADS + 1, dtype=F32) / N_HEADS)
    outs = [[] for _ in range(8)]
    for l in range(depth):
        x = _ffn(x, ffn1_norm[l], ffn1_w_gate[l], ffn1_w_up[l], ffn1_w_down[l], tm=tm_ffn, tf=tf,
                 x_tail=x_sample.reshape(dec_batch, d_model) if l == 0 else None)
        p_all = _inproj(x, mix_norm[l], _permute_in_cols(w_in[l]).astype(BF16), tm=tm, tn=IN_PAD // 3)

        mu = _permute_in_cols(jnp.concatenate([jnp.zeros((ATTN_WIDTH + 2 * KV_WIDTH,), F32), rwkv_mu[l]]))
        row = lambda a: a.reshape(1, -1)
        prep_w = (row(mu[COL_R:COL_K]), row(mu[COL_K:COL_V]), row(mu[COL_V:COL_Q]), row(mu[COL_LORA:]),
                  _lora_rows(rwkv_decay_up[l], 0), row(rwkv_decay_base[l]),
                  _lora_rows(rwkv_iclr_up[l], 64), row(rwkv_iclr_base[l]),
                  _lora_rows(rwkv_gate_up[l], 128), row(rwkv_k_k[l]), row(rwkv_k_a[l]))
        head_par = (row(rwkv_r_k[l]), row(rwkv_lnx_g[l]), row(rwkv_lnx_b[l]))

        o_rwkv_p, wkv_p = _rwkv_scan(p_all, prep_w, *head_par, batch=batch, seq=seq, tb=256)
        o_attn_p = _swa_prompt(p_all, slopes, attn_sinks[l], batch=batch, seq=seq)

        sh = _permute_in_cols(jnp.concatenate(
            [jnp.zeros((dec_batch, ATTN_WIDTH + 2 * KV_WIDTH), F32), state_rwkv_shift[l]], axis=-1))
        prep_s = _prep_sample(p_all, sh[:, COL_R:COL_K], sh[:, COL_K:COL_V], sh[:, COL_V:COL_Q], sh[:, COL_LORA:],
                              prep_w, row0=n_prompt, rows=dec_batch)
        o_rwkv_s, wkv_s = _rwkv_step(state_rwkv_wkv[l], *prep_s, *head_par, bt=8)
        p_s = p_all[n_prompt:]
        o_attn_s, kwin_s, vwin_s = _swa_sample(p_s[:, COL_Q:COL_AK], p_s[:, COL_AK:COL_AV], p_s[:, COL_AV:COL_LORA],
                                               cache_swa_k[l], cache_swa_v[l], slopes, attn_sinks[l], bt=8)

        x = _outproj(x, jnp.concatenate([o_rwkv_p, o_rwkv_s], axis=0), jnp.concatenate([o_attn_p, o_attn_s], axis=0),
                     w_out[l].astype(BF16), tm=tm)
        last_layer = l == depth - 1
        x = _ffn(x, ffn2_norm[l], ffn2_w_gate[l], ffn2_w_up[l], ffn2_w_down[l], final_norm if last_layer else None,
                 tm=tm_ffn, tf=tf, split_tail=dec_batch if last_layer else 0)

        wb = min(WINDOW, seq)
        tails = [p_all[(b + 1) * seq - wb:(b + 1) * seq] for b in range(batch)]
        last = jnp.concatenate([t[wb - 1:] for t in tails], axis=0)
        shift_cols = lambda a: jnp.concatenate([a[:, COL_R:COL_Q], a[:, COL_LORA:COL_LORA + LORA_WIDTH]], axis=-1)
        for lst, val in zip(outs, (
                shift_cols(last), wkv_p,
                jnp.stack([t[:, COL_AK:COL_AV] for t in tails]).reshape(batch, wb, N_KV_HEADS, HEAD_DIM),
                jnp.stack([t[:, COL_AV:COL_LORA] for t in tails]).reshape(batch, wb, N_KV_HEADS, HEAD_DIM),
                shift_cols(p_s), wkv_s,
                kwin_s.reshape(dec_batch, win, N_KV_HEADS, HEAD_DIM),
                vwin_s.reshape(dec_batch, win, N_KV_HEADS, HEAD_DIM))):
            lst.append(val)

    y_prompt, y_sample = x
    return ((y_prompt.reshape(batch, seq, d_model), y_sample.reshape(dec_batch, 1, d_model))
            + tuple(jnp.stack(o) for o in outs))
```

```python
import functools

import jax
import jax.numpy as jnp
from jax import lax
from jax.experimental import pallas as pl
from jax.experimental.pallas import tpu as pltpu

F32 = jnp.float32
BF16 = jnp.bfloat16
HIGHEST = lax.Precision.HIGHEST

HEAD_DIM = 64
N_HEADS = 16
N_KV_HEADS = 4
GQA_GROUP = 4
RWKV_WIDTH = 1024
ATTN_WIDTH = 1024
KV_WIDTH = 256
WINDOW = 128
LORA_WIDTH = 64 + 64 + 160
LORA_PAD = 384
RMS_EPS = 1e-6
GN_EPS = 64e-5
NEG_INF = -1e30

COL_R, COL_K, COL_V = 0, 1024, 2048
COL_Q = 3072
COL_AK = 4096
COL_AV = 4352
COL_LORA = 4608
IN_BLOCK = 512
IN_PAD = COL_LORA + IN_BLOCK

LANES = 128
CHUNK = 64
VMEM_LIMIT = 56 * 1024 * 1024


def _dot(a, b):
    return jnp.dot(a.astype(BF16), b.astype(BF16), preferred_element_type=F32)


def _dot_nt(a, b):
    return lax.dot_general(a.astype(BF16), b.astype(BF16), (((1,), (1,)), ((), ())),
                           preferred_element_type=F32)


def _dot_hi(a, b):
    return jnp.dot(a, b, preferred_element_type=F32, precision=HIGHEST)


def _rms(x, g):
    return x * lax.rsqrt(jnp.mean(x * x, axis=-1, keepdims=True) + RMS_EPS) * g


def _params(*sem):
    return pltpu.CompilerParams(dimension_semantics=sem, vmem_limit_bytes=VMEM_LIMIT)


def _ffn_kernel(*refs, final, d_ff, tail_in, tail_out):
    refs = list(refs)
    x_ref = refs.pop(0)
    xt_ref = refs.pop(0) if tail_in else None
    g_ref, wg_ref, wu_ref, wd_ref = (refs.pop(0) for _ in range(4))
    fg_ref = refs.pop(0) if final else None
    o_ref = refs.pop(0)
    ot_ref = refs.pop(0) if tail_out else None
    (xn_ref,) = refs
    f = pl.program_id(1)
    last_tile = pl.program_id(0) == pl.num_programs(0) - 1
    tm = x_ref.shape[0]
    tf = wg_ref.shape[1]
    nf = -(-d_ff // tf)
    last_width = d_ff - (nf - 1) * tf

    def start(x):
        xn_ref[...] = _rms(x, g_ref[...]).astype(BF16)
        o_ref[...] = x

    if tail_in:
        own = tm - xt_ref.shape[0]
        pl.when((f == 0) & jnp.logical_not(last_tile))(lambda: start(x_ref[...]))
        pl.when((f == 0) & last_tile)(lambda: start(jnp.concatenate([x_ref[:own, :], xt_ref[...]], axis=0)))
    else:
        pl.when(f == 0)(lambda: start(x_ref[...]))

    def accumulate(width):
        xn = xn_ref[...]
        hg = jnp.dot(xn, wg_ref[:, :width].astype(BF16), preferred_element_type=F32)
        hu = jnp.dot(xn, wu_ref[:, :width].astype(BF16), preferred_element_type=F32)
        h = (hg * jax.nn.sigmoid(hg) * (0.5 * hu)).astype(BF16)
        o_ref[...] += jnp.dot(h, wd_ref[:width, :].astype(BF16), preferred_element_type=F32)

    if last_width == tf:
        accumulate(tf)
    else:
        pl.when(f < nf - 1)(lambda: accumulate(tf))
        pl.when(f == nf - 1)(lambda: accumulate(last_width))

    if final:
        @pl.when(f == nf - 1)
        def _():
            o_ref[...] = _rms(o_ref[...], fg_ref[...])

    if tail_out:
        @pl.when((f == nf - 1) & last_tile)
        def _():
            ot_ref[...] = o_ref[tm - ot_ref.shape[0]:, :]


def _ffn(x, norm_g, wg, wu, wd, final_g=None, *, tm, tf, x_tail=None, split_tail=0):
    D = x.shape[1]
    M = x.shape[0] + (0 if x_tail is None else x_tail.shape[0])
    d_ff = wg.shape[1]
    final = final_g is not None
    const = lambda i, f: (0, 0)
    in_specs = [pl.BlockSpec((tm, D), lambda i, f: (i, 0), pipeline_mode=pl.Buffered(1))]
    args = [x]
    if x_tail is not None:
        in_specs.append(pl.BlockSpec(x_tail.shape, const))
        args.append(x_tail)
    in_specs += [pl.BlockSpec((1, D), const),
                 pl.BlockSpec((D, tf), lambda i, f: (0, f)),
                 pl.BlockSpec((D, tf), lambda i, f: (0, f)),
                 pl.BlockSpec((tf, D), lambda i, f: (f, 0))]
    args += [norm_g.reshape(1, D), wg, wu, wd]
    if final:
        in_specs.append(pl.BlockSpec((1, D), const))
        args.append(final_g.reshape(1, D))
    out_specs = [pl.BlockSpec((tm, D), lambda i, f: (i, 0))]
    out_shape = [jax.ShapeDtypeStruct((M - split_tail, D), F32)]
    if split_tail:
        out_specs.append(pl.BlockSpec((split_tail, D), const))
        out_shape.append(jax.ShapeDtypeStruct((split_tail, D), F32))
    out = pl.pallas_call(
        functools.partial(_ffn_kernel, final=final, d_ff=d_ff, tail_in=x_tail is not None,
                          tail_out=bool(split_tail)),
        grid=(M // tm, pl.cdiv(d_ff, tf)),
        in_specs=in_specs,
        out_specs=out_specs,
        out_shape=out_shape,
        scratch_shapes=[pltpu.VMEM((tm, D), BF16)],
        compiler_params=_params("arbitrary", "arbitrary"),
        name="ffn",
    )(*args)
    return out if split_tail else out[0]


def _inproj_kernel(x_ref, g_ref, w_ref, o_ref, xn_ref, *, last_valid):
    j = pl.program_id(1)
    nj = pl.num_programs(1)

    @pl.when(j == 0)
    def _():
        xn_ref[...] = _rms(x_ref[...], g_ref[...]).astype(BF16)

    def run(w):
        o_ref[...] = jnp.dot(xn_ref[...], w.astype(BF16), preferred_element_type=F32)

    col = lax.broadcasted_iota(jnp.int32, w_ref.shape, 1)
    pl.when(j < nj - 1)(lambda: run(w_ref[...]))
    pl.when(j == nj - 1)(lambda: run(jnp.where(col < last_valid, w_ref[...], 0.0)))


def _inproj(x, norm_g, w, *, tm):
    M, D = x.shape
    n_src = w.shape[1]
    tn = IN_BLOCK
    nj = IN_PAD // tn
    assert pl.cdiv(n_src, tn) == nj and COL_LORA == (nj - 1) * tn

    def src_block(j):
        return jnp.where(j < 6, j + 3, jnp.where(j < 9, j - 6, j))

    return pl.pallas_call(
        functools.partial(_inproj_kernel, last_valid=n_src - COL_LORA),
        grid=(M // tm, nj),
        in_specs=[pl.BlockSpec((tm, D), lambda i, j: (i, 0), pipeline_mode=pl.Buffered(1)),
                  pl.BlockSpec((1, D), lambda i, j: (0, 0)),
                  pl.BlockSpec((D, tn), lambda i, j: (0, src_block(j)))],
        out_specs=pl.BlockSpec((tm, tn), lambda i, j: (i, j)),
        out_shape=jax.ShapeDtypeStruct((M, IN_PAD), F32),
        scratch_shapes=[pltpu.VMEM((tm, D), BF16)],
        compiler_params=_params("parallel", "arbitrary"),
        name="inproj",
    )(x, norm_g.reshape(1, D), w)


def _outproj_kernel(x_ref, oa_ref, ob_ref, oat_ref, obt_ref, w_ref, o_ref):
    half = oa_ref.shape[1]
    own = x_ref.shape[0] - oat_ref.shape[0]
    last_tile = pl.program_id(0) == pl.num_programs(0) - 1

    def run(oa, ob):
        o_ref[...] = (x_ref[...]
                      + jnp.dot(oa.astype(BF16), w_ref[:half, :], preferred_element_type=F32)
                      + jnp.dot(ob.astype(BF16), w_ref[half:, :], preferred_element_type=F32))

    pl.when(jnp.logical_not(last_tile))(lambda: run(oa_ref[...], ob_ref[...]))
    pl.when(last_tile)(lambda: run(jnp.concatenate([oa_ref[:own, :], oat_ref[...]], axis=0),
                                   jnp.concatenate([ob_ref[:own, :], obt_ref[...]], axis=0)))


def _outproj(x, o_rwkv, o_attn, o_rwkv_tail, o_attn_tail, w, *, tm):
    M, D = x.shape
    Wd = o_rwkv.shape[1]
    tail = pl.BlockSpec(o_rwkv_tail.shape, lambda i: (0, 0))
    return pl.pallas_call(
        _outproj_kernel,
        grid=(M // tm,),
        in_specs=[pl.BlockSpec((tm, D), lambda i: (i, 0)),
                  pl.BlockSpec((tm, Wd), lambda i: (i, 0)),
                  pl.BlockSpec((tm, Wd), lambda i: (i, 0)),
                  tail, tail,
                  pl.BlockSpec(w.shape, lambda i: (0, 0))],
        out_specs=pl.BlockSpec((tm, D), lambda i: (i, 0)),
        out_shape=jax.ShapeDtypeStruct((M, D), F32),
        compiler_params=_params("arbitrary"),
        name="outproj",
    )(x, o_rwkv, o_attn, o_rwkv_tail, o_attn_tail, w)


def _head_ones():
    r = lax.broadcasted_iota(jnp.int32, (LANES, LANES), 0) // HEAD_DIM
    c = lax.broadcasted_iota(jnp.int32, (LANES, LANES), 1) // HEAD_DIM
    return (r == c).astype(F32)


def _head_sum(x):
    ones = _head_ones().astype(BF16)
    terms = _split_terms(x, 2)
    parts = [sum(jnp.dot(t[:, j * LANES:(j + 1) * LANES], ones, preferred_element_type=F32) for t in terms)
             for j in range(x.shape[1] // LANES)]
    return parts[0] if len(parts) == 1 else jnp.concatenate(parts, axis=-1)


def _rwkv_prep_math(p_r, p_k, p_v, p_l, q_r, q_k, q_v, q_l, w):
    (mu_r, mu_k, mu_v, mu_l, dec_up, dec_base, icl_up, icl_base, gate_up, k_k, k_a) = w
    r = p_r + mu_r * (q_r - p_r)
    k = p_k + mu_k * (q_k - p_k)
    v = p_v + mu_v * (q_v - p_v)
    ul = p_l + mu_l * (q_l - p_l)
    w_log = -jax.nn.softplus(-(dec_base + _dot(jnp.tanh(ul), dec_up))) - 0.5
    lw = -jnp.exp(w_log)
    a = jax.nn.sigmoid(icl_base + _dot(ul, icl_up))
    g = _dot(jax.nn.sigmoid(ul), gate_up)
    kk = k * k_k
    kk = kk / jnp.maximum(jnp.sqrt(_head_sum(kk * kk)), 1e-12)
    k2 = k * (1.0 + (a - 1.0) * k_a)
    return r, lw, k2, v, kk, kk * a, g


def _prep_sample_kernel(pr, pk, pv, pL, qr, qk, qv, qL, *rest):
    w_refs, out_refs = rest[:11], rest[11:]
    outs = _rwkv_prep_math(pr[...], pk[...], pv[...], pL[...], qr[...], qk[...], qv[...], qL[...],
                           tuple(r[...] for r in w_refs))
    for o_ref, o in zip(out_refs, outs):
        o_ref[...] = o


def _prep_weight_specs(w, nargs):
    zero = (lambda b, i: (0, 0)) if nargs == 2 else (lambda i: (0, 0))
    return [pl.BlockSpec(a.shape, zero) for a in w]


def _prep_sample(p_all, shift_r, shift_k, shift_v, shift_l, w, *, row0, rows):
    W = RWKV_WIDTH
    rb = row0 // rows

    def cur(col_block, width):
        return pl.BlockSpec((rows, width), lambda i: (rb, col_block))

    def full(width):
        return pl.BlockSpec((rows, width), lambda i: (0, 0))

    in_specs = [cur(0, W), cur(1, W), cur(2, W), cur(COL_LORA // LORA_PAD, LORA_PAD),
                full(W), full(W), full(W), full(LORA_PAD)]
    in_specs += _prep_weight_specs(w, 1)
    out = jax.ShapeDtypeStruct((rows, W), F32)
    return pl.pallas_call(
        _prep_sample_kernel,
        grid=(1,),
        in_specs=in_specs,
        out_specs=[pl.BlockSpec((rows, W), lambda i: (0, 0))] * 7,
        out_shape=[out] * 7,
        compiler_params=_params("arbitrary"),
        name="rwkv_prep_sample",
    )(p_all, p_all, p_all, p_all, shift_r, shift_k, shift_v, shift_l, *w)


def _stack_heads(x):
    lane = lax.broadcasted_iota(jnp.int32, x.shape, 1)
    return jnp.concatenate([jnp.where(lane < HEAD_DIM, x, 0.0), jnp.where(lane >= HEAD_DIM, x, 0.0)], axis=0)


def _split_terms(x, n):
    terms = []
    for _ in range(n):
        hi = x.astype(BF16)
        terms.append(hi)
        x = x - hi.astype(F32)
    return terms


def _rwkv_chunk_pair(rg, nag, bi, ki, bend, kend, v, etot, st, consts):
    C = CHUNK
    strict, incl, eye = consts
    rg, nag, bi, ki, bend, kend, vs = (_stack_heads(a) for a in (rg, nag, bi, ki, bend, kend, v))

    gram = _dot_nt(jnp.concatenate([nag, rg], axis=0), jnp.concatenate([bi, ki], axis=0))
    yield
    a_ab = jnp.where(strict, gram[:2 * C, :2 * C], 0.0)
    a_ak = jnp.where(strict, gram[:2 * C, 2 * C:], 0.0)
    a_rb = jnp.where(incl, gram[2 * C:, :2 * C], 0.0)
    a_rk = jnp.where(incl, gram[2 * C:, 2 * C:], 0.0)

    z = _dot_nt(nag, st) + _dot(a_ak, vs)
    o2 = _dot_nt(rg, st) + _dot(a_rk, vs)
    yield

    pw = a_ab
    inv = eye + pw
    span = 1
    while span * 2 < C:
        pw = _dot(pw, pw)
        yield
        inv = inv + _dot(inv, pw)
        span *= 2
    yield

    u = _dot(inv, z)
    yield
    o2 = o2 + _dot(a_rb, u)
    st_new = st * etot + _dot(u.T, bend) + _dot(vs.T, kend)
    yield o2[:C, :] + o2[C:, :], st_new


def _rwkv_scan_kernel(pr, pk, pv, pL, qr, qk, qv, qL, *rest):
    w_refs = rest[:11]
    rk_ref, lng_ref, lnb_ref, o_ref, s_out_ref = rest[11:16]
    chunk_refs = rest[16:24]
    bonus_ref, gate_ref, st_ref = rest[24:27]
    C = CHUNK
    n_chunks = pr.shape[0] // C
    pairs = pr.shape[1] // LANES
    first = pl.program_id(1) == 0

    @pl.when(first)
    def _():
        st_ref[...] = jnp.zeros_like(st_ref)

    def prev_rows(cur_ref, tail_ref):
        cur = cur_ref[...]
        tail = jnp.where(first, 0.0, tail_ref[7:8, :])
        row0 = lax.broadcasted_iota(jnp.int32, cur.shape, 0) == 0
        return jnp.where(row0, tail, pltpu.roll(cur, 1, axis=0))

    r, lw, k, v, kk, b, g = _rwkv_prep_math(
        pr[...], pk[...], pv[...], pL[...],
        prev_rows(pr, qr), prev_rows(pk, qk), prev_rows(pv, qv), prev_rows(pL, qL),
        tuple(ref[...] for ref in w_refs))

    tb = pr.shape[0]
    ti = lax.broadcasted_iota(jnp.int32, (tb, tb), 0)
    tj = lax.broadcasted_iota(jnp.int32, (tb, tb), 1)
    same_chunk = (ti // C) == (tj // C)
    lw_terms = _split_terms(lw, 3)
    tri = (same_chunk & (tj <= ti)).astype(BF16)
    cum = sum(jnp.dot(tri, t, preferred_element_type=F32) for t in lw_terms)
    tot = sum(jnp.dot(same_chunk.astype(BF16), t, preferred_element_type=F32) for t in lw_terms)
    ginv = jnp.exp(-cum)
    gend = jnp.exp(tot - cum)
    scaled = (r * jnp.exp(cum), -kk * jnp.exp(cum - lw), b * ginv, k * ginv, b * gend, k * gend, v, jnp.exp(tot))
    for ref, val in zip(chunk_refs, scaled):
        ref[...] = val
    bonus_ref[...] = _head_sum(r * k * rk_ref[...]) * v
    gate_ref[...] = g

    row = lax.broadcasted_iota(jnp.int32, (2 * C, 2 * C), 0)
    col = lax.broadcasted_iota(jnp.int32, (2 * C, 2 * C), 1)
    consts = (col < row, col <= row, (col == row).astype(F32))

    def chunk(c, carry):
        rows = pl.ds(pl.multiple_of(c * C, C), C)
        def pair(j):
            lanes = slice(j * LANES, (j + 1) * LANES)
            tok = [ref[rows, lanes] for ref in chunk_refs]
            tok[-1] = tok[-1][:1, :]
            return _rwkv_chunk_pair(*tok, st_ref[j], consts)

        stages = [pair(j) for j in range(pairs)]
        results = [None] * pairs
        while results[-1] is None:
            for j, gen in enumerate(stages):
                results[j] = next(gen)
        for j, (o, st_new) in enumerate(results):
            o_ref[rows, j * LANES:(j + 1) * LANES] = o
            st_ref[j] = st_new
        return carry

    lax.fori_loop(0, n_chunks, chunk, 0)

    o = o_ref[...]
    mean = _head_sum(o) * (1.0 / HEAD_DIM)
    d = o - mean
    var = _head_sum(d * d) * (1.0 / HEAD_DIM)
    o_ref[...] = (d * lax.rsqrt(var + GN_EPS) * lng_ref[...] + lnb_ref[...] + bonus_ref[...]) * gate_ref[...]

    @pl.when(pl.program_id(1) == pl.num_programs(1) - 1)
    def _():
        for j in range(pairs):
            st = st_ref[j]
            s_out_ref[0, 2 * j] = st[:HEAD_DIM, :HEAD_DIM]
            s_out_ref[0, 2 * j + 1] = st[HEAD_DIM:, HEAD_DIM:]


def _rwkv_scan(p_all, prep_w, r_k, lnx_g, lnx_b, *, batch, seq, tb):
    nt = seq // tb
    W = RWKV_WIDTH
    pairs = W // LANES

    def cur(col_block, width):
        return pl.BlockSpec((tb, width), lambda b, i: (b * nt + i, col_block))

    def tail(col_block, width):
        return pl.BlockSpec((8, width), lambda b, i: (jnp.maximum((b * seq + i * tb) // 8 - 1, 0), col_block))

    lora_block = COL_LORA // LORA_PAD
    in_specs = [cur(0, W), cur(1, W), cur(2, W), cur(lora_block, LORA_PAD),
                tail(0, W), tail(1, W), tail(2, W), tail(lora_block, LORA_PAD)]
    in_specs += _prep_weight_specs(prep_w, 2)
    in_specs += [pl.BlockSpec((1, W), lambda b, i: (0, 0))] * 3
    return pl.pallas_call(
        _rwkv_scan_kernel,
        grid=(batch, nt),
        in_specs=in_specs,
        out_specs=[pl.BlockSpec((tb, W), lambda b, i: (b * nt + i, 0)),
                   pl.BlockSpec((1, N_HEADS, HEAD_DIM, HEAD_DIM), lambda b, i: (b, 0, 0, 0))],
        out_shape=[jax.ShapeDtypeStruct((batch * seq, W), F32),
                   jax.ShapeDtypeStruct((batch, N_HEADS, HEAD_DIM, HEAD_DIM), F32)],
        scratch_shapes=[pltpu.VMEM((tb, W), F32)] * 10 + [pltpu.VMEM((pairs, LANES, LANES), F32)],
        compiler_params=_params("parallel", "arbitrary"),
        name="rwkv_scan",
    )(*([p_all] * 8), *prep_w, r_k, lnx_g, lnx_b)


def _rwkv_step_kernel(s_ref, r_ref, lw_ref, k_ref, v_ref, kk_ref, b_ref, g_ref, rk_ref, lng_ref, lnb_ref,
                      o_ref, s_out_ref):
    s = s_ref[...]
    r, k, v, kk, b, g = (ref[...] for ref in (r_ref, k_ref, v_ref, kk_ref, b_ref, g_ref))
    w = jnp.exp(lw_ref[...])
    shape = (HEAD_DIM, HEAD_DIM)
    eye = lax.broadcasted_iota(jnp.int32, shape, 0) == lax.broadcasted_iota(jnp.int32, shape, 1)
    sa = jnp.sum(s * (-kk), axis=-1, keepdims=True)
    v_col = jnp.sum(jnp.where(eye, v, 0.0), axis=-1, keepdims=True)
    s_new = s * w + sa * b + v_col * k
    s_out_ref[...] = s_new
    o_col = jnp.sum(s_new * r, axis=-1, keepdims=True)
    o = jnp.sum(jnp.where(eye, o_col, 0.0), axis=-2, keepdims=True)
    mean = jnp.mean(o, axis=-1, keepdims=True)
    d = o - mean
    var = jnp.mean(d * d, axis=-1, keepdims=True)
    on = d * lax.rsqrt(var + GN_EPS) * lng_ref[...] + lnb_ref[...]
    bonus = jnp.sum(r * k * rk_ref[...], axis=-1, keepdims=True) * v
    o_ref[...] = (on + bonus) * g


def _rwkv_step(state, r, lw, k, v, kk, b, g, r_k, lnx_g, lnx_b, *, bt):
    B = state.shape[0]
    vec4 = lambda a: a.reshape(B, N_HEADS, 1, HEAD_DIM)
    par4 = lambda a: a.reshape(1, N_HEADS, 1, HEAD_DIM)
    st_spec = pl.BlockSpec((bt, N_HEADS, HEAD_DIM, HEAD_DIM), lambda i: (i, 0, 0, 0))
    vec_spec = pl.BlockSpec((bt, N_HEADS, 1, HEAD_DIM), lambda i: (i, 0, 0, 0))
    par_spec = pl.BlockSpec((1, N_HEADS, 1, HEAD_DIM), lambda i: (0, 0, 0, 0))
    o, s_new = pl.pallas_call(
        _rwkv_step_kernel,
        grid=(B // bt,),
        in_specs=[st_spec] + [vec_spec] * 7 + [par_spec] * 3,
        out_specs=[vec_spec, st_spec],
        out_shape=[jax.ShapeDtypeStruct((B, N_HEADS, 1, HEAD_DIM), F32),
                   jax.ShapeDtypeStruct(state.shape, F32)],
        compiler_params=_params("parallel"),
        name="rwkv_step",
    )(state, *(vec4(a) for a in (r, lw, k, v, kk, b, g)), par4(r_k), par4(lnx_g), par4(lnx_b))
    return o.reshape(B, RWKV_WIDTH), s_new


def _swa_prompt_kernel(q_ref, kc_ref, kp_ref, vc_ref, vp_ref, slope_ref, sink_ref, o_ref):
    blk = q_ref.shape[0]
    has_prev = pl.program_id(2) > 0
    kj = lax.broadcasted_iota(jnp.int32, (2 * blk, blk), 0)
    qi = lax.broadcasted_iota(jnp.int32, (2 * blk, blk), 1) + blk
    dist = qi - kj
    valid = (dist >= 0) & (dist < WINDOW) & (has_prev | (kj >= blk))
    distf = dist.astype(F32)

    kcat = jnp.concatenate([kp_ref[...], kc_ref[...]], axis=0)
    kroll = pltpu.roll(kcat, HEAD_DIM, axis=1)
    low_lane = lax.broadcasted_iota(jnp.int32, kcat.shape, 1) < HEAD_DIM
    vt = jnp.concatenate([vp_ref[...], vc_ref[...]], axis=0).T
    vroll = pltpu.roll(vt, HEAD_DIM, axis=0)
    low_row = lax.broadcasted_iota(jnp.int32, vt.shape, 0) < HEAD_DIM
    kv = [(jnp.where(low_lane, kcat, 0.0).astype(BF16), jnp.where(low_lane, 0.0, kroll).astype(BF16),
           jnp.where(low_row, vt, 0.0).astype(BF16), jnp.where(low_row, 0.0, vroll).astype(BF16)),
          (jnp.where(low_lane, kroll, 0.0).astype(BF16), jnp.where(low_lane, 0.0, kcat).astype(BF16),
           jnp.where(low_row, vroll, 0.0).astype(BF16), jnp.where(low_row, 0.0, vt).astype(BF16))]
    out_low_row = lax.broadcasted_iota(jnp.int32, (LANES, blk), 0) < HEAD_DIM

    def softmax_t(s, h):
        s = jnp.where(valid, s - slope_ref[h][:, :blk] * distf, NEG_INF)
        sink = sink_ref[h][:, :blk]
        m = jnp.maximum(jnp.max(s, axis=0, keepdims=True), sink)
        e = jnp.exp(s - m)
        return e, jnp.sum(e, axis=0, keepdims=True) + jnp.exp(sink - m)

    def head_pair(i):
        k_lo, k_hi, v_lo, v_hi = kv[i // 2]
        cols = slice(i * LANES, (i + 1) * LANES)
        q = (q_ref[:, cols] * (HEAD_DIM ** -0.5)).astype(BF16)
        s_lo = lax.dot_general(k_lo, q, (((1,), (1,)), ((), ())), preferred_element_type=F32)
        s_hi = lax.dot_general(k_hi, q, (((1,), (1,)), ((), ())), preferred_element_type=F32)
        yield
        e_lo, d_lo = softmax_t(s_lo, 2 * i)
        e_hi, d_hi = softmax_t(s_hi, 2 * i + 1)
        ot = (jnp.dot(v_lo, e_lo.astype(BF16), preferred_element_type=F32)
              + jnp.dot(v_hi, e_hi.astype(BF16), preferred_element_type=F32))
        yield
        o_ref[:, cols] = (ot / jnp.where(out_low_row, d_lo, d_hi)).T
        yield

    pairs = [head_pair(i) for i in range(q_ref.shape[1] // LANES)]
    for _ in range(3):
        for gen in pairs:
            next(gen)


def _swa_prompt(p_all, slopes, sinks, *, batch, seq):
    blk = WINDOW
    nb = seq // blk
    qw = 2 * GQA_GROUP * HEAD_DIM
    cur = lambda col: (lambda b, j, i: (b * nb + i, col + j))
    prev = lambda col: (lambda b, j, i: (b * nb + jnp.maximum(i - 1, 0), col + j))
    par = pl.BlockSpec((2 * GQA_GROUP, 1, 2 * blk), lambda b, j, i: (j, 0, 0))
    return pl.pallas_call(
        _swa_prompt_kernel,
        grid=(batch, N_KV_HEADS // 2, nb),
        in_specs=[pl.BlockSpec((blk, qw), cur(COL_Q // qw)),
                  pl.BlockSpec((blk, LANES), cur(COL_AK // LANES)),
                  pl.BlockSpec((blk, LANES), prev(COL_AK // LANES)),
                  pl.BlockSpec((blk, LANES), cur(COL_AV // LANES)),
                  pl.BlockSpec((blk, LANES), prev(COL_AV // LANES)),
                  par, par],
        out_specs=pl.BlockSpec((blk, qw), lambda b, j, i: (b * nb + i, j)),
        out_shape=jax.ShapeDtypeStruct((batch * seq, ATTN_WIDTH), F32),
        compiler_params=_params("parallel", "parallel", "arbitrary"),
        name="swa_prompt",
    )(p_all, p_all, p_all, p_all, p_all,
      jnp.broadcast_to(slopes.reshape(N_HEADS, 1, 1), (N_HEADS, 1, 2 * blk)),
      jnp.broadcast_to(sinks.reshape(N_HEADS, 1, 1), (N_HEADS, 1, 2 * blk)))


def _swa_sample_kernel(q_ref, kn_ref, vn_ref, ck_ref, cv_ref, slope_ref, sink_ref, o_ref, kw_ref, vw_ref):
    bt = q_ref.shape[0]
    win = ck_ref.shape[1]
    row = lax.broadcasted_iota(jnp.int32, (win, KV_WIDTH), 0)
    head_kv = lax.broadcasted_iota(jnp.int32, (N_HEADS, KV_WIDTH), 0) // GQA_GROUP
    lane_kv = lax.broadcasted_iota(jnp.int32, (N_HEADS, KV_WIDTH), 1) // HEAD_DIM
    own = head_kv == lane_kv
    dist = (win - 1 - lax.broadcasted_iota(jnp.int32, (N_HEADS, win), 1)).astype(F32)
    bias = slope_ref[...] * dist
    sink = sink_ref[...][:, :1]
    for i in range(bt):
        keys = jnp.where(row == win - 1, kn_ref[i:i + 1, :], pltpu.roll(ck_ref[i], win - 1, axis=0))
        vals = jnp.where(row == win - 1, vn_ref[i:i + 1, :], pltpu.roll(cv_ref[i], win - 1, axis=0))
        kw_ref[i] = keys
        vw_ref[i] = vals
        q = q_ref[i]
        qbd = jnp.where(own, jnp.concatenate([q] * N_KV_HEADS, axis=-1), 0.0)
        s = _dot_nt(qbd, keys) * (HEAD_DIM ** -0.5) - bias
        m = jnp.maximum(jnp.max(s, axis=-1, keepdims=True), sink)
        e = jnp.exp(s - m)
        denom = jnp.sum(e, axis=-1, keepdims=True) + jnp.exp(sink - m)
        o2 = jnp.where(own, _dot(e, vals), 0.0)
        o = o2[:, :HEAD_DIM]
        for c in range(1, N_KV_HEADS):
            o = o + o2[:, c * HEAD_DIM:(c + 1) * HEAD_DIM]
        o_ref[i] = o / denom


def _swa_sample(q, k_new, v_new, cache_k, cache_v, slopes, sinks, *, bt):
    B, win = cache_k.shape[0], cache_k.shape[1]
    q3 = pl.BlockSpec((bt, N_HEADS, HEAD_DIM), lambda i: (i, 0, 0))
    new = pl.BlockSpec((bt, KV_WIDTH), lambda i: (i, 0))
    cache = pl.BlockSpec((bt, win, KV_WIDTH), lambda i: (i, 0, 0))
    par = pl.BlockSpec((N_HEADS, win), lambda i: (0, 0))
    o, kw, vw = pl.pallas_call(
        _swa_sample_kernel,
        grid=(B // bt,),
        in_specs=[q3, new, new, cache, cache, par, par],
        out_specs=[q3, cache, cache],
        out_shape=[jax.ShapeDtypeStruct((B, N_HEADS, HEAD_DIM), F32),
                   jax.ShapeDtypeStruct((B, win, KV_WIDTH), F32),
                   jax.ShapeDtypeStruct((B, win, KV_WIDTH), F32)],
        compiler_params=_params("parallel"),
        name="swa_sample",
    )(q.reshape(B, N_HEADS, HEAD_DIM), k_new, v_new,
      cache_k.reshape(B, win, KV_WIDTH), cache_v.reshape(B, win, KV_WIDTH),
      jnp.broadcast_to(slopes.reshape(N_HEADS, 1), (N_HEADS, win)),
      jnp.broadcast_to(sinks.reshape(N_HEADS, 1), (N_HEADS, win)))
    return o.reshape(B, ATTN_WIDTH), kw, vw


def _permute_in_cols(a):
    attn_proj = ATTN_WIDTH + 2 * KV_WIDTH
    rkv = a[..., attn_proj:attn_proj + 3 * RWKV_WIDTH]
    lora = a[..., attn_proj + 3 * RWKV_WIDTH:]
    pad = jnp.zeros(a.shape[:-1] + (LORA_PAD - LORA_WIDTH,), a.dtype)
    return jnp.concatenate([rkv, a[..., :attn_proj], lora, pad], axis=-1)


def _lora_rows(w, start):
    return jnp.zeros((LORA_PAD, w.shape[1]), F32).at[start:start + w.shape[0]].set(w).astype(BF16)


def kernel(x_prompt, x_sample, state_rwkv_shift, state_rwkv_wkv, cache_swa_k, cache_swa_v, ffn1_norm, ffn1_w_gate, ffn1_w_up, ffn1_w_down, mix_norm, w_in, rwkv_mu, rwkv_decay_up, rwkv_decay_base, rwkv_iclr_up, rwkv_iclr_base, rwkv_gate_up, rwkv_k_k, rwkv_k_a, rwkv_r_k, rwkv_lnx_g, rwkv_lnx_b, attn_sinks, w_out, ffn2_norm, ffn2_w_gate, ffn2_w_up, ffn2_w_down, final_norm):
    batch, seq, d_model = x_prompt.shape
    dec_batch = x_sample.shape[0]
    depth = ffn1_norm.shape[0]
    n_prompt = batch * seq
    win = cache_swa_k.shape[2]
    assert x_sample.shape[1] == 1 and seq % 512 == 0 and win == WINDOW and n_prompt % dec_batch == 0
    tm, tm_ffn, tf, tm_in = 640, 832, 512, 1664
    assert all((n_prompt + dec_batch) % t == 0 for t in (tm, tm_ffn, tm_in))

    x = x_prompt.reshape(n_prompt, d_model)
    slopes = jnp.exp2(-8.0 * jnp.arange(1, N_HEADS + 1, dtype=F32) / N_HEADS)
    outs = [[] for _ in range(8)]
    for l in range(depth):
        x = _ffn(x, ffn1_norm[l], ffn1_w_gate[l], ffn1_w_up[l], ffn1_w_down[l], tm=tm_ffn, tf=tf,
                 x_tail=x_sample.reshape(dec_batch, d_model) if l == 0 else None)
        p_all = _inproj(x, mix_norm[l], w_in[l], tm=tm_in)

        mu = _permute_in_cols(jnp.concatenate([jnp.zeros((ATTN_WIDTH + 2 * KV_WIDTH,), F32), rwkv_mu[l]]))
        row = lambda a: a.reshape(1, -1)
        prep_w = (row(mu[COL_R:COL_K]), row(mu[COL_K:COL_V]), row(mu[COL_V:COL_Q]), row(mu[COL_LORA:]),
                  _lora_rows(rwkv_decay_up[l], 0), row(rwkv_decay_base[l]),
                  _lora_rows(rwkv_iclr_up[l], 64), row(rwkv_iclr_base[l]),
                  _lora_rows(rwkv_gate_up[l], 128), row(rwkv_k_k[l]), row(rwkv_k_a[l]))
        head_par = (row(rwkv_r_k[l]), row(rwkv_lnx_g[l]), row(rwkv_lnx_b[l]))

        o_rwkv_p, wkv_p = _rwkv_scan(p_all, prep_w, *head_par, batch=batch, seq=seq, tb=256)
        o_attn_p = _swa_prompt(p_all, slopes, attn_sinks[l], batch=batch, seq=seq)

        sh = _permute_in_cols(jnp.concatenate(
            [jnp.zeros((dec_batch, ATTN_WIDTH + 2 * KV_WIDTH), F32), state_rwkv_shift[l]], axis=-1))
        prep_s = _prep_sample(p_all, sh[:, COL_R:COL_K], sh[:, COL_K:COL_V], sh[:, COL_V:COL_Q], sh[:, COL_LORA:],
                              prep_w, row0=n_prompt, rows=dec_batch)
        o_rwkv_s, wkv_s = _rwkv_step(state_rwkv_wkv[l], *prep_s, *head_par, bt=8)
        p_s = p_all[n_prompt:]
        o_attn_s, kwin_s, vwin_s = _swa_sample(p_s[:, COL_Q:COL_AK], p_s[:, COL_AK:COL_AV], p_s[:, COL_AV:COL_LORA],
                                               cache_swa_k[l], cache_swa_v[l], slopes, attn_sinks[l], bt=8)

        x = _outproj(x, o_rwkv_p, o_attn_p, o_rwkv_s, o_attn_s, w_out[l].astype(BF16), tm=tm)
        last_layer = l == depth - 1
        x = _ffn(x, ffn2_norm[l], ffn2_w_gate[l], ffn2_w_up[l], ffn2_w_down[l], final_norm if last_layer else None,
                 tm=tm_ffn, tf=tf, split_tail=dec_batch if last_layer else 0)

        wb = min(WINDOW, seq)
        tails = [p_all[(b + 1) * seq - wb:(b + 1) * seq] for b in range(batch)]
        last = jnp.concatenate([t[wb - 1:] for t in tails], axis=0)
        shift_cols = lambda a: jnp.concatenate([a[:, COL_R:COL_Q], a[:, COL_LORA:COL_LORA + LORA_WIDTH]], axis=-1)
        for lst, val in zip(outs, (
                shift_cols(last), wkv_p,
                jnp.stack([t[:, COL_AK:COL_AV] for t in tails]).reshape(batch, wb, N_KV_HEADS, HEAD_DIM),
                jnp.stack([t[:, COL_AV:COL_LORA] for t in tails]).reshape(batch, wb, N_KV_HEADS, HEAD_DIM),
                shift_cols(p_s), wkv_s,
                kwin_s.reshape(dec_batch, win, N_KV_HEADS, HEAD_DIM),
                vwin_s.reshape(dec_batch, win, N_KV_HEADS, HEAD_DIM))):
            lst.append(val)

    y_prompt, y_sample = x
    return ((y_prompt.reshape(batch, seq, d_model), y_sample.reshape(dec_batch, 1, d_model))
            + tuple(jnp.stack(o) for o in outs))
```

```python
import functools

import jax
import jax.numpy as jnp
from jax import lax
from jax.experimental import pallas as pl
from jax.experimental.pallas import tpu as pltpu

F32 = jnp.float32
BF16 = jnp.bfloat16
HIGHEST = lax.Precision.HIGHEST

HEAD_DIM = 64
N_HEADS = 16
N_KV_HEADS = 4
GQA_GROUP = 4
RWKV_WIDTH = 1024
ATTN_WIDTH = 1024
KV_WIDTH = 256
WINDOW = 128
LORA_WIDTH = 64 + 64 + 160
LORA_PAD = 384
RMS_EPS = 1e-6
GN_EPS = 64e-5
NEG_INF = -1e30

COL_R, COL_K, COL_V = 0, 1024, 2048
COL_Q = 3072
COL_AK = 4096
COL_AV = 4352
COL_LORA = 4608
IN_BLOCK = 512
IN_PAD = COL_LORA + IN_BLOCK

LANES = 128
CHUNK = 64
VMEM_LIMIT = 56 * 1024 * 1024


def _dot(a, b):
    return jnp.dot(a.astype(BF16), b.astype(BF16), preferred_element_type=F32)


def _dot_nt(a, b):
    return lax.dot_general(a.astype(BF16), b.astype(BF16), (((1,), (1,)), ((), ())),
                           preferred_element_type=F32)


def _dot_hi(a, b):
    return jnp.dot(a, b, preferred_element_type=F32, precision=HIGHEST)


def _rms(x, g):
    return x * lax.rsqrt(jnp.mean(x * x, axis=-1, keepdims=True) + RMS_EPS) * g


def _params(*sem):
    return pltpu.CompilerParams(dimension_semantics=sem, vmem_limit_bytes=VMEM_LIMIT)


def _ffn_kernel(*refs, final, d_ff, tail_in, tail_out):
    refs = list(refs)
    x_ref = refs.pop(0)
    xt_ref = refs.pop(0) if tail_in else None
    g_ref, wg_ref, wu_ref, wd_ref = (refs.pop(0) for _ in range(4))
    fg_ref = refs.pop(0) if final else None
    o_ref = refs.pop(0)
    ot_ref = refs.pop(0) if tail_out else None
    (xn_ref,) = refs
    f = pl.program_id(1)
    last_tile = pl.program_id(0) == pl.num_programs(0) - 1
    tm = x_ref.shape[0]
    tf = wg_ref.shape[1]
    nf = -(-d_ff // tf)
    last_width = d_ff - (nf - 1) * tf

    def start(x):
        xn_ref[...] = _rms(x, g_ref[...]).astype(BF16)
        o_ref[...] = x

    if tail_in:
        own = tm - xt_ref.shape[0]
        pl.when((f == 0) & jnp.logical_not(last_tile))(lambda: start(x_ref[...]))
        pl.when((f == 0) & last_tile)(lambda: start(jnp.concatenate([x_ref[:own, :], xt_ref[...]], axis=0)))
    else:
        pl.when(f == 0)(lambda: start(x_ref[...]))

    def accumulate(width):
        xn = xn_ref[...]
        hg = jnp.dot(xn, wg_ref[:, :width].astype(BF16), preferred_element_type=F32)
        hu = jnp.dot(xn, wu_ref[:, :width].astype(BF16), preferred_element_type=F32)
        h = (hg * jax.nn.sigmoid(hg) * (0.5 * hu)).astype(BF16)
        o_ref[...] += jnp.dot(h, wd_ref[:width, :].astype(BF16), preferred_element_type=F32)

    if last_width == tf:
        accumulate(tf)
    else:
        pl.when(f < nf - 1)(lambda: accumulate(tf))
        pl.when(f == nf - 1)(lambda: accumulate(last_width))

    if final:
        @pl.when(f == nf - 1)
        def _():
            o_ref[...] = _rms(o_ref[...], fg_ref[...])

    if tail_out:
        @pl.when((f == nf - 1) & last_tile)
        def _():
            ot_ref[...] = o_ref[tm - ot_ref.shape[0]:, :]


def _ffn(x, norm_g, wg, wu, wd, final_g=None, *, tm, tf, x_tail=None, split_tail=0):
    D = x.shape[1]
    M = x.shape[0] + (0 if x_tail is None else x_tail.shape[0])
    d_ff = wg.shape[1]
    final = final_g is not None
    const = lambda i, f: (0, 0)
    in_specs = [pl.BlockSpec((tm, D), lambda i, f: (i, 0), pipeline_mode=pl.Buffered(1))]
    args = [x]
    if x_tail is not None:
        in_specs.append(pl.BlockSpec(x_tail.shape, const))
        args.append(x_tail)
    in_specs += [pl.BlockSpec((1, D), const),
                 pl.BlockSpec((D, tf), lambda i, f: (0, f)),
                 pl.BlockSpec((D, tf), lambda i, f: (0, f)),
                 pl.BlockSpec((tf, D), lambda i, f: (f, 0))]
    args += [norm_g.reshape(1, D), wg, wu, wd]
    if final:
        in_specs.append(pl.BlockSpec((1, D), const))
        args.append(final_g.reshape(1, D))
    out_specs = [pl.BlockSpec((tm, D), lambda i, f: (i, 0))]
    out_shape = [jax.ShapeDtypeStruct((M - split_tail, D), F32)]
    if split_tail:
        out_specs.append(pl.BlockSpec((split_tail, D), const))
        out_shape.append(jax.ShapeDtypeStruct((split_tail, D), F32))
    out = pl.pallas_call(
        functools.partial(_ffn_kernel, final=final, d_ff=d_ff, tail_in=x_tail is not None,
                          tail_out=bool(split_tail)),
        grid=(M // tm, pl.cdiv(d_ff, tf)),
        in_specs=in_specs,
        out_specs=out_specs,
        out_shape=out_shape,
        scratch_shapes=[pltpu.VMEM((tm, D), BF16)],
        compiler_params=_params("arbitrary", "arbitrary"),
        name="ffn",
    )(*args)
    return out if split_tail else out[0]


def _inproj_kernel(x_ref, g_ref, w_ref, o_ref, xn_ref, *, last_valid):
    j = pl.program_id(1)
    nj = pl.num_programs(1)

    @pl.when(j == 0)
    def _():
        xn_ref[...] = _rms(x_ref[...], g_ref[...]).astype(BF16)

    def run(wt):
        o_ref[...] = lax.dot_general(xn_ref[...], wt.astype(BF16), (((1,), (1,)), ((), ())),
                                     preferred_element_type=F32)

    row = lax.broadcasted_iota(jnp.int32, w_ref.shape, 0)
    pl.when(j < nj - 1)(lambda: run(w_ref[...]))
    pl.when(j == nj - 1)(lambda: run(jnp.where(row < last_valid, w_ref[...], 0.0)))


def _inproj(x, norm_g, w_t, *, tm):
    M, D = x.shape
    n_src = w_t.shape[0]
    tn = IN_BLOCK
    nj = IN_PAD // tn
    assert pl.cdiv(n_src, tn) == nj and COL_LORA == (nj - 1) * tn

    def src_block(j):
        return jnp.where(j < 6, j + 3, jnp.where(j < 9, j - 6, j))

    return pl.pallas_call(
        functools.partial(_inproj_kernel, last_valid=n_src - COL_LORA),
        grid=(M // tm, nj),
        in_specs=[pl.BlockSpec((tm, D), lambda i, j: (i, 0)),
                  pl.BlockSpec((1, D), lambda i, j: (0, 0)),
                  pl.BlockSpec((tn, D), lambda i, j: (src_block(j), 0))],
        out_specs=pl.BlockSpec((tm, tn), lambda i, j: (i, j)),
        out_shape=jax.ShapeDtypeStruct((M, IN_PAD), F32),
        scratch_shapes=[pltpu.VMEM((tm, D), BF16)],
        compiler_params=_params("parallel", "arbitrary"),
        name="inproj",
    )(x, norm_g.reshape(1, D), w_t)


def _outproj_kernel(x_ref, oa_ref, ob_ref, oat_ref, obt_ref, w_ref, o_ref):
    half = oa_ref.shape[1]
    own = x_ref.shape[0] - oat_ref.shape[0]
    last_tile = pl.program_id(0) == pl.num_programs(0) - 1

    def run(oa, ob):
        o_ref[...] = (x_ref[...]
                      + jnp.dot(oa.astype(BF16), w_ref[:half, :], preferred_element_type=F32)
                      + jnp.dot(ob.astype(BF16), w_ref[half:, :], preferred_element_type=F32))

    pl.when(jnp.logical_not(last_tile))(lambda: run(oa_ref[...], ob_ref[...]))
    pl.when(last_tile)(lambda: run(jnp.concatenate([oa_ref[:own, :], oat_ref[...]], axis=0),
                                   jnp.concatenate([ob_ref[:own, :], obt_ref[...]], axis=0)))


def _outproj(x, o_rwkv, o_attn, o_rwkv_tail, o_attn_tail, w, *, tm):
    M, D = x.shape
    Wd = o_rwkv.shape[1]
    tail = pl.BlockSpec(o_rwkv_tail.shape, lambda i: (0, 0))
    return pl.pallas_call(
        _outproj_kernel,
        grid=(M // tm,),
        in_specs=[pl.BlockSpec((tm, D), lambda i: (i, 0)),
                  pl.BlockSpec((tm, Wd), lambda i: (i, 0)),
                  pl.BlockSpec((tm, Wd), lambda i: (i, 0)),
                  tail, tail,
                  pl.BlockSpec(w.shape, lambda i: (0, 0))],
        out_specs=pl.BlockSpec((tm, D), lambda i: (i, 0)),
        out_shape=jax.ShapeDtypeStruct((M, D), F32),
        compiler_params=_params("arbitrary"),
        name="outproj",
    )(x, o_rwkv, o_attn, o_rwkv_tail, o_attn_tail, w)


def _head_ones():
    r = lax.broadcasted_iota(jnp.int32, (LANES, LANES), 0) // HEAD_DIM
    c = lax.broadcasted_iota(jnp.int32, (LANES, LANES), 1) // HEAD_DIM
    return (r == c).astype(F32)


def _head_sum(x):
    ones = _head_ones().astype(BF16)
    terms = _split_terms(x, 2)
    parts = [sum(jnp.dot(t[:, j * LANES:(j + 1) * LANES], ones, preferred_element_type=F32) for t in terms)
             for j in range(x.shape[1] // LANES)]
    return parts[0] if len(parts) == 1 else jnp.concatenate(parts, axis=-1)


def _rwkv_prep_math(p_r, p_k, p_v, p_l, q_r, q_k, q_v, q_l, w):
    (mu_r, mu_k, mu_v, mu_l, dec_up, dec_base, icl_up, icl_base, gate_up, k_k, k_a) = w
    r = p_r + mu_r * (q_r - p_r)
    k = p_k + mu_k * (q_k - p_k)
    v = p_v + mu_v * (q_v - p_v)
    ul = p_l + mu_l * (q_l - p_l)
    w_log = -jax.nn.softplus(-(dec_base + _dot(jnp.tanh(ul), dec_up))) - 0.5
    lw = -jnp.exp(w_log)
    a = jax.nn.sigmoid(icl_base + _dot(ul, icl_up))
    g = _dot(jax.nn.sigmoid(ul), gate_up)
    kk = k * k_k
    kk = kk / jnp.maximum(jnp.sqrt(_head_sum(kk * kk)), 1e-12)
    k2 = k * (1.0 + (a - 1.0) * k_a)
    return r, lw, k2, v, kk, kk * a, g


def _prep_sample_kernel(pr, pk, pv, pL, qr, qk, qv, qL, *rest):
    w_refs, out_refs = rest[:11], rest[11:]
    outs = _rwkv_prep_math(pr[...], pk[...], pv[...], pL[...], qr[...], qk[...], qv[...], qL[...],
                           tuple(r[...] for r in w_refs))
    for o_ref, o in zip(out_refs, outs):
        o_ref[...] = o


def _prep_weight_specs(w, nargs):
    zero = (lambda b, i: (0, 0)) if nargs == 2 else (lambda i: (0, 0))
    return [pl.BlockSpec(a.shape, zero) for a in w]


def _prep_sample(p_all, shift_r, shift_k, shift_v, shift_l, w, *, row0, rows):
    W = RWKV_WIDTH
    rb = row0 // rows

    def cur(col_block, width):
        return pl.BlockSpec((rows, width), lambda i: (rb, col_block))

    def full(width):
        return pl.BlockSpec((rows, width), lambda i: (0, 0))

    in_specs = [cur(0, W), cur(1, W), cur(2, W), cur(COL_LORA // LORA_PAD, LORA_PAD),
                full(W), full(W), full(W), full(LORA_PAD)]
    in_specs += _prep_weight_specs(w, 1)
    out = jax.ShapeDtypeStruct((rows, W), F32)
    return pl.pallas_call(
        _prep_sample_kernel,
        grid=(1,),
        in_specs=in_specs,
        out_specs=[pl.BlockSpec((rows, W), lambda i: (0, 0))] * 7,
        out_shape=[out] * 7,
        compiler_params=_params("arbitrary"),
        name="rwkv_prep_sample",
    )(p_all, p_all, p_all, p_all, shift_r, shift_k, shift_v, shift_l, *w)


def _stack_heads(x):
    lane = lax.broadcasted_iota(jnp.int32, x.shape, 1)
    return jnp.concatenate([jnp.where(lane < HEAD_DIM, x, 0.0), jnp.where(lane >= HEAD_DIM, x, 0.0)], axis=0)


def _split_terms(x, n):
    terms = []
    for _ in range(n):
        hi = x.astype(BF16)
        terms.append(hi)
        x = x - hi.astype(F32)
    return terms


def _rwkv_chunk_pair(rg, nag, bi, ki, bend, kend, v, etot, st, consts):
    C = CHUNK
    strict, incl, eye = consts
    rg, nag, bi, ki, bend, kend, vs = (_stack_heads(a) for a in (rg, nag, bi, ki, bend, kend, v))

    gram = _dot_nt(jnp.concatenate([nag, rg], axis=0), jnp.concatenate([bi, ki], axis=0))
    yield
    a_ab = jnp.where(strict, gram[:2 * C, :2 * C], 0.0)
    a_ak = jnp.where(strict, gram[:2 * C, 2 * C:], 0.0)
    a_rb = jnp.where(incl, gram[2 * C:, :2 * C], 0.0)
    a_rk = jnp.where(incl, gram[2 * C:, 2 * C:], 0.0)

    z = _dot_nt(nag, st) + _dot(a_ak, vs)
    o2 = _dot_nt(rg, st) + _dot(a_rk, vs)
    yield

    pw = a_ab
    inv = eye + pw
    span = 1
    while span * 2 < C:
        pw = _dot(pw, pw)
        yield
        inv = inv + _dot(inv, pw)
        span *= 2
    yield

    u = _dot(inv, z)
    yield
    o2 = o2 + _dot(a_rb, u)
    st_new = st * etot + _dot(u.T, bend) + _dot(vs.T, kend)
    yield o2[:C, :] + o2[C:, :], st_new


def _rwkv_scan_kernel(pr, pk, pv, pL, qr, qk, qv, qL, *rest):
    w_refs = rest[:11]
    rk_ref, lng_ref, lnb_ref, o_ref, s_out_ref = rest[11:16]
    chunk_refs = rest[16:24]
    bonus_ref, gate_ref, st_ref = rest[24:27]
    C = CHUNK
    n_chunks = pr.shape[0] // C
    pairs = pr.shape[1] // LANES
    first = pl.program_id(1) == 0

    @pl.when(first)
    def _():
        st_ref[...] = jnp.zeros_like(st_ref)

    def prev_rows(cur_ref, tail_ref):
        cur = cur_ref[...]
        tail = jnp.where(first, 0.0, tail_ref[7:8, :])
        row0 = lax.broadcasted_iota(jnp.int32, cur.shape, 0) == 0
        return jnp.where(row0, tail, pltpu.roll(cur, 1, axis=0))

    r, lw, k, v, kk, b, g = _rwkv_prep_math(
        pr[...], pk[...], pv[...], pL[...],
        prev_rows(pr, qr), prev_rows(pk, qk), prev_rows(pv, qv), prev_rows(pL, qL),
        tuple(ref[...] for ref in w_refs))

    tb = pr.shape[0]
    ti = lax.broadcasted_iota(jnp.int32, (tb, tb), 0)
    tj = lax.broadcasted_iota(jnp.int32, (tb, tb), 1)
    same_chunk = (ti // C) == (tj // C)
    lw_terms = _split_terms(lw, 3)
    tri = (same_chunk & (tj <= ti)).astype(BF16)
    cum = sum(jnp.dot(tri, t, preferred_element_type=F32) for t in lw_terms)
    tot = sum(jnp.dot(same_chunk.astype(BF16), t, preferred_element_type=F32) for t in lw_terms)
    ginv = jnp.exp(-cum)
    gend = jnp.exp(tot - cum)
    scaled = (r * jnp.exp(cum), -kk * jnp.exp(cum - lw), b * ginv, k * ginv, b * gend, k * gend, v, jnp.exp(tot))
    for ref, val in zip(chunk_refs, scaled):
        ref[...] = val
    bonus_ref[...] = _head_sum(r * k * rk_ref[...]) * v
    gate_ref[...] = g

    row = lax.broadcasted_iota(jnp.int32, (2 * C, 2 * C), 0)
    col = lax.broadcasted_iota(jnp.int32, (2 * C, 2 * C), 1)
    consts = (col < row, col <= row, (col == row).astype(F32))

    def chunk(c, carry):
        rows = pl.ds(pl.multiple_of(c * C, C), C)
        def pair(j):
            lanes = slice(j * LANES, (j + 1) * LANES)
            tok = [ref[rows, lanes] for ref in chunk_refs]
            tok[-1] = tok[-1][:1, :]
            return _rwkv_chunk_pair(*tok, st_ref[j], consts)

        stages = [pair(j) for j in range(pairs)]
        results = [None] * pairs
        while results[-1] is None:
            for j, gen in enumerate(stages):
                results[j] = next(gen)
        for j, (o, st_new) in enumerate(results):
            o_ref[rows, j * LANES:(j + 1) * LANES] = o
            st_ref[j] = st_new
        return carry

    lax.fori_loop(0, n_chunks, chunk, 0)

    o = o_ref[...]
    mean = _head_sum(o) * (1.0 / HEAD_DIM)
    d = o - mean
    var = _head_sum(d * d) * (1.0 / HEAD_DIM)
    o_ref[...] = (d * lax.rsqrt(var + GN_EPS) * lng_ref[...] + lnb_ref[...] + bonus_ref[...]) * gate_ref[...]

    @pl.when(pl.program_id(1) == pl.num_programs(1) - 1)
    def _():
        for j in range(pairs):
            st = st_ref[j]
            s_out_ref[0, 2 * j] = st[:HEAD_DIM, :HEAD_DIM]
            s_out_ref[0, 2 * j + 1] = st[HEAD_DIM:, HEAD_DIM:]


def _rwkv_scan(p_all, prep_w, r_k, lnx_g, lnx_b, *, batch, seq, tb):
    nt = seq // tb
    W = RWKV_WIDTH
    pairs = W // LANES

    def cur(col_block, width):
        return pl.BlockSpec((tb, width), lambda b, i: (b * nt + i, col_block))

    def tail(col_block, width):
        return pl.BlockSpec((8, width), lambda b, i: (jnp.maximum((b * seq + i * tb) // 8 - 1, 0), col_block))

    lora_block = COL_LORA // LORA_PAD
    in_specs = [cur(0, W), cur(1, W), cur(2, W), cur(lora_block, LORA_PAD),
                tail(0, W), tail(1, W), tail(2, W), tail(lora_block, LORA_PAD)]
    in_specs += _prep_weight_specs(prep_w, 2)
    in_specs += [pl.BlockSpec((1, W), lambda b, i: (0, 0))] * 3
    return pl.pallas_call(
        _rwkv_scan_kernel,
        grid=(batch, nt),
        in_specs=in_specs,
        out_specs=[pl.BlockSpec((tb, W), lambda b, i: (b * nt + i, 0)),
                   pl.BlockSpec((1, N_HEADS, HEAD_DIM, HEAD_DIM), lambda b, i: (b, 0, 0, 0))],
        out_shape=[jax.ShapeDtypeStruct((batch * seq, W), F32),
                   jax.ShapeDtypeStruct((batch, N_HEADS, HEAD_DIM, HEAD_DIM), F32)],
        scratch_shapes=[pltpu.VMEM((tb, W), F32)] * 10 + [pltpu.VMEM((pairs, LANES, LANES), F32)],
        compiler_params=_params("parallel", "arbitrary"),
        name="rwkv_scan",
    )(*([p_all] * 8), *prep_w, r_k, lnx_g, lnx_b)


def _rwkv_step_kernel(s_ref, r_ref, lw_ref, k_ref, v_ref, kk_ref, b_ref, g_ref, rk_ref, lng_ref, lnb_ref,
                      o_ref, s_out_ref):
    s = s_ref[...]
    r, k, v, kk, b, g = (ref[...] for ref in (r_ref, k_ref, v_ref, kk_ref, b_ref, g_ref))
    w = jnp.exp(lw_ref[...])
    shape = (HEAD_DIM, HEAD_DIM)
    eye = lax.broadcasted_iota(jnp.int32, shape, 0) == lax.broadcasted_iota(jnp.int32, shape, 1)
    sa = jnp.sum(s * (-kk), axis=-1, keepdims=True)
    v_col = jnp.sum(jnp.where(eye, v, 0.0), axis=-1, keepdims=True)
    s_new = s * w + sa * b + v_col * k
    s_out_ref[...] = s_new
    o_col = jnp.sum(s_new * r, axis=-1, keepdims=True)
    o = jnp.sum(jnp.where(eye, o_col, 0.0), axis=-2, keepdims=True)
    mean = jnp.mean(o, axis=-1, keepdims=True)
    d = o - mean
    var = jnp.mean(d * d, axis=-1, keepdims=True)
    on = d * lax.rsqrt(var + GN_EPS) * lng_ref[...] + lnb_ref[...]
    bonus = jnp.sum(r * k * rk_ref[...], axis=-1, keepdims=True) * v
    o_ref[...] = (on + bonus) * g


def _rwkv_step(state, r, lw, k, v, kk, b, g, r_k, lnx_g, lnx_b, *, bt):
    B = state.shape[0]
    vec4 = lambda a: a.reshape(B, N_HEADS, 1, HEAD_DIM)
    par4 = lambda a: a.reshape(1, N_HEADS, 1, HEAD_DIM)
    st_spec = pl.BlockSpec((bt, N_HEADS, HEAD_DIM, HEAD_DIM), lambda i: (i, 0, 0, 0))
    vec_spec = pl.BlockSpec((bt, N_HEADS, 1, HEAD_DIM), lambda i: (i, 0, 0, 0))
    par_spec = pl.BlockSpec((1, N_HEADS, 1, HEAD_DIM), lambda i: (0, 0, 0, 0))
    o, s_new = pl.pallas_call(
        _rwkv_step_kernel,
        grid=(B // bt,),
        in_specs=[st_spec] + [vec_spec] * 7 + [par_spec] * 3,
        out_specs=[vec_spec, st_spec],
        out_shape=[jax.ShapeDtypeStruct((B, N_HEADS, 1, HEAD_DIM), F32),
                   jax.ShapeDtypeStruct(state.shape, F32)],
        compiler_params=_params("parallel"),
        name="rwkv_step",
    )(state, *(vec4(a) for a in (r, lw, k, v, kk, b, g)), par4(r_k), par4(lnx_g), par4(lnx_b))
    return o.reshape(B, RWKV_WIDTH), s_new


def _swa_prompt_kernel(q_ref, kc_ref, kp_ref, vc_ref, vp_ref, slope_ref, sink_ref, o_ref):
    blk = q_ref.shape[0]
    has_prev = pl.program_id(2) > 0
    kj = lax.broadcasted_iota(jnp.int32, (2 * blk, blk), 0)
    qi = lax.broadcasted_iota(jnp.int32, (2 * blk, blk), 1) + blk
    dist = qi - kj
    valid = (dist >= 0) & (dist < WINDOW) & (has_prev | (kj >= blk))
    distf = dist.astype(F32)

    kcat = jnp.concatenate([kp_ref[...], kc_ref[...]], axis=0)
    kroll = pltpu.roll(kcat, HEAD_DIM, axis=1)
    low_lane = lax.broadcasted_iota(jnp.int32, kcat.shape, 1) < HEAD_DIM
    vt = jnp.concatenate([vp_ref[...], vc_ref[...]], axis=0).T
    vroll = pltpu.roll(vt, HEAD_DIM, axis=0)
    low_row = lax.broadcasted_iota(jnp.int32, vt.shape, 0) < HEAD_DIM
    kv = [(jnp.where(low_lane, kcat, 0.0).astype(BF16), jnp.where(low_lane, 0.0, kroll).astype(BF16),
           jnp.where(low_row, vt, 0.0).astype(BF16), jnp.where(low_row, 0.0, vroll).astype(BF16)),
          (jnp.where(low_lane, kroll, 0.0).astype(BF16), jnp.where(low_lane, 0.0, kcat).astype(BF16),
           jnp.where(low_row, vroll, 0.0).astype(BF16), jnp.where(low_row, 0.0, vt).astype(BF16))]
    out_low_row = lax.broadcasted_iota(jnp.int32, (LANES, blk), 0) < HEAD_DIM

    def softmax_t(s, h):
        s = jnp.where(valid, s - slope_ref[h][:, :blk] * distf, NEG_INF)
        sink = sink_ref[h][:, :blk]
        m = jnp.maximum(jnp.max(s, axis=0, keepdims=True), sink)
        e = jnp.exp(s - m)
        return e, jnp.sum(e, axis=0, keepdims=True) + jnp.exp(sink - m)

    def head_pair(i):
        k_lo, k_hi, v_lo, v_hi = kv[i // 2]
        cols = slice(i * LANES, (i + 1) * LANES)
        q = (q_ref[:, cols] * (HEAD_DIM ** -0.5)).astype(BF16)
        s_lo = lax.dot_general(k_lo, q, (((1,), (1,)), ((), ())), preferred_element_type=F32)
        s_hi = lax.dot_general(k_hi, q, (((1,), (1,)), ((), ())), preferred_element_type=F32)
        yield
        e_lo, d_lo = softmax_t(s_lo, 2 * i)
        e_hi, d_hi = softmax_t(s_hi, 2 * i + 1)
        ot = (jnp.dot(v_lo, e_lo.astype(BF16), preferred_element_type=F32)
              + jnp.dot(v_hi, e_hi.astype(BF16), preferred_element_type=F32))
        yield
        o_ref[:, cols] = (ot / jnp.where(out_low_row, d_lo, d_hi)).T
        yield

    pairs = [head_pair(i) for i in range(q_ref.shape[1] // LANES)]
    for _ in range(3):
        for gen in pairs:
            next(gen)


def _swa_prompt(p_all, slopes, sinks, *, batch, seq):
    blk = WINDOW
    nb = seq // blk
    qw = 2 * GQA_GROUP * HEAD_DIM
    cur = lambda col: (lambda b, j, i: (b * nb + i, col + j))
    prev = lambda col: (lambda b, j, i: (b * nb + jnp.maximum(i - 1, 0), col + j))
    par = pl.BlockSpec((2 * GQA_GROUP, 1, 2 * blk), lambda b, j, i: (j, 0, 0))
    return pl.pallas_call(
        _swa_prompt_kernel,
        grid=(batch, N_KV_HEADS // 2, nb),
        in_specs=[pl.BlockSpec((blk, qw), cur(COL_Q // qw)),
                  pl.BlockSpec((blk, LANES), cur(COL_AK // LANES)),
                  pl.BlockSpec((blk, LANES), prev(COL_AK // LANES)),
                  pl.BlockSpec((blk, LANES), cur(COL_AV // LANES)),
                  pl.BlockSpec((blk, LANES), prev(COL_AV // LANES)),
                  par, par],
        out_specs=pl.BlockSpec((blk, qw), lambda b, j, i: (b * nb + i, j)),
        out_shape=jax.ShapeDtypeStruct((batch * seq, ATTN_WIDTH), F32),
        compiler_params=_params("parallel", "parallel", "arbitrary"),
        name="swa_prompt",
    )(p_all, p_all, p_all, p_all, p_all,
      jnp.broadcast_to(slopes.reshape(N_HEADS, 1, 1), (N_HEADS, 1, 2 * blk)),
      jnp.broadcast_to(sinks.reshape(N_HEADS, 1, 1), (N_HEADS, 1, 2 * blk)))


def _swa_sample_kernel(q_ref, kn_ref, vn_ref, ck_ref, cv_ref, slope_ref, sink_ref, o_ref, kw_ref, vw_ref):
    bt = q_ref.shape[0]
    win = ck_ref.shape[1]
    row = lax.broadcasted_iota(jnp.int32, (win, KV_WIDTH), 0)
    head_kv = lax.broadcasted_iota(jnp.int32, (N_HEADS, KV_WIDTH), 0) // GQA_GROUP
    lane_kv = lax.broadcasted_iota(jnp.int32, (N_HEADS, KV_WIDTH), 1) // HEAD_DIM
    own = head_kv == lane_kv
    dist = (win - 1 - lax.broadcasted_iota(jnp.int32, (N_HEADS, win), 1)).astype(F32)
    bias = slope_ref[...] * dist
    sink = sink_ref[...][:, :1]
    for i in range(bt):
        keys = jnp.where(row == win - 1, kn_ref[i:i + 1, :], pltpu.roll(ck_ref[i], win - 1, axis=0))
        vals = jnp.where(row == win - 1, vn_ref[i:i + 1, :], pltpu.roll(cv_ref[i], win - 1, axis=0))
        kw_ref[i] = keys
        vw_ref[i] = vals
        q = q_ref[i]
        qbd = jnp.where(own, jnp.concatenate([q] * N_KV_HEADS, axis=-1), 0.0)
        s = _dot_nt(qbd, keys) * (HEAD_DIM ** -0.5) - bias
        m = jnp.maximum(jnp.max(s, axis=-1, keepdims=True), sink)
        e = jnp.exp(s - m)
        denom = jnp.sum(e, axis=-1, keepdims=True) + jnp.exp(sink - m)
        o2 = jnp.where(own, _dot(e, vals), 0.0)
        o = o2[:, :HEAD_DIM]
        for c in range(1, N_KV_HEADS):
            o = o + o2[:, c * HEAD_DIM:(c + 1) * HEAD_DIM]
        o_ref[i] = o / denom


def _swa_sample(q, k_new, v_new, cache_k, cache_v, slopes, sinks, *, bt):
    B, win = cache_k.shape[0], cache_k.shape[1]
    q3 = pl.BlockSpec((bt, N_HEADS, HEAD_DIM), lambda i: (i, 0, 0))
    new = pl.BlockSpec((bt, KV_WIDTH), lambda i: (i, 0))
    cache = pl.BlockSpec((bt, win, KV_WIDTH), lambda i: (i, 0, 0))
    par = pl.BlockSpec((N_HEADS, win), lambda i: (0, 0))
    o, kw, vw = pl.pallas_call(
        _swa_sample_kernel,
        grid=(B // bt,),
        in_specs=[q3, new, new, cache, cache, par, par],
        out_specs=[q3, cache, cache],
        out_shape=[jax.ShapeDtypeStruct((B, N_HEADS, HEAD_DIM), F32),
                   jax.ShapeDtypeStruct((B, win, KV_WIDTH), F32),
                   jax.ShapeDtypeStruct((B, win, KV_WIDTH), F32)],
        compiler_params=_params("parallel"),
        name="swa_sample",
    )(q.reshape(B, N_HEADS, HEAD_DIM), k_new, v_new,
      cache_k.reshape(B, win, KV_WIDTH), cache_v.reshape(B, win, KV_WIDTH),
      jnp.broadcast_to(slopes.reshape(N_HEADS, 1), (N_HEADS, win)),
      jnp.broadcast_to(sinks.reshape(N_HEADS, 1), (N_HEADS, win)))
    return o.reshape(B, ATTN_WIDTH), kw, vw


def _permute_in_cols(a):
    attn_proj = ATTN_WIDTH + 2 * KV_WIDTH
    rkv = a[..., attn_proj:attn_proj + 3 * RWKV_WIDTH]
    lora = a[..., attn_proj + 3 * RWKV_WIDTH:]
    pad = jnp.zeros(a.shape[:-1] + (LORA_PAD - LORA_WIDTH,), a.dtype)
    return jnp.concatenate([rkv, a[..., :attn_proj], lora, pad], axis=-1)


def _lora_rows(w, start):
    return jnp.zeros((LORA_PAD, w.shape[1]), F32).at[start:start + w.shape[0]].set(w).astype(BF16)


def kernel(x_prompt, x_sample, state_rwkv_shift, state_rwkv_wkv, cache_swa_k, cache_swa_v, ffn1_norm, ffn1_w_gate, ffn1_w_up, ffn1_w_down, mix_norm, w_in, rwkv_mu, rwkv_decay_up, rwkv_decay_base, rwkv_iclr_up, rwkv_iclr_base, rwkv_gate_up, rwkv_k_k, rwkv_k_a, rwkv_r_k, rwkv_lnx_g, rwkv_lnx_b, attn_sinks, w_out, ffn2_norm, ffn2_w_gate, ffn2_w_up, ffn2_w_down, final_norm):
    batch, seq, d_model = x_prompt.shape
    dec_batch = x_sample.shape[0]
    depth = ffn1_norm.shape[0]
    n_prompt = batch * seq
    win = cache_swa_k.shape[2]
    assert x_sample.shape[1] == 1 and seq % 512 == 0 and win == WINDOW and n_prompt % dec_batch == 0
    tm, tm_ffn, tf, tm_in = 640, 832, 512, 832
    assert all((n_prompt + dec_batch) % t == 0 for t in (tm, tm_ffn, tm_in))

    x = x_prompt.reshape(n_prompt, d_model)
    slopes = jnp.exp2(-8.0 * jnp.arange(1, N_HEADS + 1, dtype=F32) / N_HEADS)
    outs = [[] for _ in range(8)]
    for l in range(depth):
        x = _ffn(x, ffn1_norm[l], ffn1_w_gate[l], ffn1_w_up[l], ffn1_w_down[l], tm=tm_ffn, tf=tf,
                 x_tail=x_sample.reshape(dec_batch, d_model) if l == 0 else None)
        p_all = _inproj(x, mix_norm[l], w_in[l].T, tm=tm_in)

        mu = _permute_in_cols(jnp.concatenate([jnp.zeros((ATTN_WIDTH + 2 * KV_WIDTH,), F32), rwkv_mu[l]]))
        row = lambda a: a.reshape(1, -1)
        prep_w = (row(mu[COL_R:COL_K]), row(mu[COL_K:COL_V]), row(mu[COL_V:COL_Q]), row(mu[COL_LORA:]),
                  _lora_rows(rwkv_decay_up[l], 0), row(rwkv_decay_base[l]),
                  _lora_rows(rwkv_iclr_up[l], 64), row(rwkv_iclr_base[l]),
                  _lora_rows(rwkv_gate_up[l], 128), row(rwkv_k_k[l]), row(rwkv_k_a[l]))
        head_par = (row(rwkv_r_k[l]), row(rwkv_lnx_g[l]), row(rwkv_lnx_b[l]))

        o_rwkv_p, wkv_p = _rwkv_scan(p_all, prep_w, *head_par, batch=batch, seq=seq, tb=256)
        o_attn_p = _swa_prompt(p_all, slopes, attn_sinks[l], batch=batch, seq=seq)

        sh = _permute_in_cols(jnp.concatenate(
            [jnp.zeros((dec_batch, ATTN_WIDTH + 2 * KV_WIDTH), F32), state_rwkv_shift[l]], axis=-1))
        prep_s = _prep_sample(p_all, sh[:, COL_R:COL_K], sh[:, COL_K:COL_V], sh[:, COL_V:COL_Q], sh[:, COL_LORA:],
                              prep_w, row0=n_prompt, rows=dec_batch)
        o_rwkv_s, wkv_s = _rwkv_step(state_rwkv_wkv[l], *prep_s, *head_par, bt=8)
        p_s = p_all[n_prompt:]
        o_attn_s, kwin_s, vwin_s = _swa_sample(p_s[:, COL_Q:COL_AK], p_s[:, COL_AK:COL_AV], p_s[:, COL_AV:COL_LORA],
                                               cache_swa_k[l], cache_swa_v[l], slopes, attn_sinks[l], bt=8)

        x = _outproj(x, o_rwkv_p, o_attn_p, o_rwkv_s, o_attn_s, w_out[l].astype(BF16), tm=tm)
        last_layer = l == depth - 1
        x = _ffn(x, ffn2_norm[l], ffn2_w_gate[l], ffn2_w_up[l], ffn2_w_down[l], final_norm if last_layer else None,
                 tm=tm_ffn, tf=tf, split_tail=dec_batch if last_layer else 0)

        wb = min(WINDOW, seq)
        tails = [p_all[(b + 1) * seq - wb:(b + 1) * seq] for b in range(batch)]
        last = jnp.concatenate([t[wb - 1:] for t in tails], axis=0)
        shift_cols = lambda a: jnp.concatenate([a[:, COL_R:COL_Q], a[:, COL_LORA:COL_LORA + LORA_WIDTH]], axis=-1)
        for lst, val in zip(outs, (
                shift_cols(last), wkv_p,
                jnp.stack([t[:, COL_AK:COL_AV] for t in tails]).reshape(batch, wb, N_KV_HEADS, HEAD_DIM),
                jnp.stack([t[:, COL_AV:COL_LORA] for t in tails]).reshape(batch, wb, N_KV_HEADS, HEAD_DIM),
                shift_cols(p_s), wkv_s,
                kwin_s.reshape(dec_batch, win, N_KV_HEADS, HEAD_DIM),
                vwin_s.reshape(dec_batch, win, N_KV_HEADS, HEAD_DIM))):
            lst.append(val)

    y_prompt, y_sample = x
    return ((y_prompt.reshape(batch, seq, d_model), y_sample.reshape(dec_batch, 1, d_model))
            + tuple(jnp.stack(o) for o in outs))
```

```python
import functools

import jax
import jax.numpy as jnp
from jax import lax
from jax.experimental import pallas as pl
from jax.experimental.pallas import tpu as pltpu

F32 = jnp.float32
BF16 = jnp.bfloat16
HIGHEST = lax.Precision.HIGHEST

HEAD_DIM = 64
N_HEADS = 16
N_KV_HEADS = 4
GQA_GROUP = 4
RWKV_WIDTH = 1024
ATTN_WIDTH = 1024
KV_WIDTH = 256
WINDOW = 128
LORA_WIDTH = 64 + 64 + 160
LORA_PAD = 384
RMS_EPS = 1e-6
GN_EPS = 64e-5
NEG_INF = -1e30

COL_R, COL_K, COL_V = 0, 1024, 2048
COL_Q = 3072
COL_AK = 4096
COL_AV = 4352
COL_LORA = 4608
IN_BLOCK = 512
IN_PAD = COL_LORA + IN_BLOCK

LANES = 128
CHUNK = 64
VMEM_LIMIT = 56 * 1024 * 1024


def _dot(a, b):
    return jnp.dot(a.astype(BF16), b.astype(BF16), preferred_element_type=F32)


def _dot_nt(a, b):
    return lax.dot_general(a.astype(BF16), b.astype(BF16), (((1,), (1,)), ((), ())),
                           preferred_element_type=F32)


def _dot_hi(a, b):
    return jnp.dot(a, b, preferred_element_type=F32, precision=HIGHEST)


def _rms(x, g):
    return x * lax.rsqrt(jnp.mean(x * x, axis=-1, keepdims=True) + RMS_EPS) * g


def _params(*sem):
    return pltpu.CompilerParams(dimension_semantics=sem, vmem_limit_bytes=VMEM_LIMIT)


def _ffn_kernel(*refs, final, d_ff, tail_in, tail_out):
    refs = list(refs)
    x_ref = refs.pop(0)
    xt_ref = refs.pop(0) if tail_in else None
    g_ref, wg_ref, wu_ref, wd_ref = (refs.pop(0) for _ in range(4))
    fg_ref = refs.pop(0) if final else None
    o_ref = refs.pop(0)
    ot_ref = refs.pop(0) if tail_out else None
    (xn_ref,) = refs
    f = pl.program_id(1)
    last_tile = pl.program_id(0) == pl.num_programs(0) - 1
    tm = x_ref.shape[0]
    tf = wg_ref.shape[1]
    nf = -(-d_ff // tf)
    last_width = d_ff - (nf - 1) * tf

    def start(x):
        xn_ref[...] = _rms(x, g_ref[...]).astype(BF16)
        o_ref[...] = x

    if tail_in:
        own = tm - xt_ref.shape[0]
        pl.when((f == 0) & jnp.logical_not(last_tile))(lambda: start(x_ref[...]))
        pl.when((f == 0) & last_tile)(lambda: start(jnp.concatenate([x_ref[:own, :], xt_ref[...]], axis=0)))
    else:
        pl.when(f == 0)(lambda: start(x_ref[...]))

    def accumulate(width):
        xn = xn_ref[...]
        hg = jnp.dot(xn, wg_ref[:, :width].astype(BF16), preferred_element_type=F32)
        hu = jnp.dot(xn, wu_ref[:, :width].astype(BF16), preferred_element_type=F32)
        h = (hg * jax.nn.sigmoid(hg) * (0.5 * hu)).astype(BF16)
        o_ref[...] += jnp.dot(h, wd_ref[:width, :].astype(BF16), preferred_element_type=F32)

    if last_width == tf:
        accumulate(tf)
    else:
        pl.when(f < nf - 1)(lambda: accumulate(tf))
        pl.when(f == nf - 1)(lambda: accumulate(last_width))

    if final:
        @pl.when(f == nf - 1)
        def _():
            o_ref[...] = _rms(o_ref[...], fg_ref[...])

    if tail_out:
        @pl.when((f == nf - 1) & last_tile)
        def _():
            ot_ref[...] = o_ref[tm - ot_ref.shape[0]:, :]


def _ffn(x, norm_g, wg, wu, wd, final_g=None, *, tm, tf, x_tail=None, split_tail=0):
    D = x.shape[1]
    M = x.shape[0] + (0 if x_tail is None else x_tail.shape[0])
    d_ff = wg.shape[1]
    final = final_g is not None
    const = lambda i, f: (0, 0)
    in_specs = [pl.BlockSpec((tm, D), lambda i, f: (i, 0), pipeline_mode=pl.Buffered(1))]
    args = [x]
    if x_tail is not None:
        in_specs.append(pl.BlockSpec(x_tail.shape, const))
        args.append(x_tail)
    in_specs += [pl.BlockSpec((1, D), const),
                 pl.BlockSpec((D, tf), lambda i, f: (0, f)),
                 pl.BlockSpec((D, tf), lambda i, f: (0, f)),
                 pl.BlockSpec((tf, D), lambda i, f: (f, 0))]
    args += [norm_g.reshape(1, D), wg, wu, wd]
    if final:
        in_specs.append(pl.BlockSpec((1, D), const))
        args.append(final_g.reshape(1, D))
    out_specs = [pl.BlockSpec((tm, D), lambda i, f: (i, 0))]
    out_shape = [jax.ShapeDtypeStruct((M - split_tail, D), F32)]
    if split_tail:
        out_specs.append(pl.BlockSpec((split_tail, D), const))
        out_shape.append(jax.ShapeDtypeStruct((split_tail, D), F32))
    out = pl.pallas_call(
        functools.partial(_ffn_kernel, final=final, d_ff=d_ff, tail_in=x_tail is not None,
                          tail_out=bool(split_tail)),
        grid=(M // tm, pl.cdiv(d_ff, tf)),
        in_specs=in_specs,
        out_specs=out_specs,
        out_shape=out_shape,
        scratch_shapes=[pltpu.VMEM((tm, D), BF16)],
        compiler_params=_params("arbitrary", "arbitrary"),
        name="ffn",
    )(*args)
    return out if split_tail else out[0]


def _inproj_kernel(x_ref, g_ref, w_ref, o_ref, xn_ref, *, last_valid):
    j = pl.program_id(1)
    nj = pl.num_programs(1)

    @pl.when(j == 0)
    def _():
        xn_ref[...] = _rms(x_ref[...], g_ref[...]).astype(BF16)

    def run(wt):
        o_ref[...] = lax.dot_general(xn_ref[...], wt.astype(BF16), (((1,), (1,)), ((), ())),
                                     preferred_element_type=F32)

    row = lax.broadcasted_iota(jnp.int32, w_ref.shape, 0)
    pl.when(j < nj - 1)(lambda: run(w_ref[...]))
    pl.when(j == nj - 1)(lambda: run(jnp.where(row < last_valid, w_ref[...], 0.0)))


def _inproj(x, norm_g, w_t, *, tm):
    M, D = x.shape
    n_src = w_t.shape[0]
    tn = IN_BLOCK
    nj = IN_PAD // tn
    assert pl.cdiv(n_src, tn) == nj and COL_LORA == (nj - 1) * tn

    def src_block(j):
        return jnp.where(j < 6, j + 3, jnp.where(j < 9, j - 6, j))

    return pl.pallas_call(
        functools.partial(_inproj_kernel, last_valid=n_src - COL_LORA),
        grid=(M // tm, nj),
        in_specs=[pl.BlockSpec((tm, D), lambda i, j: (i, 0), pipeline_mode=pl.Buffered(1)),
                  pl.BlockSpec((1, D), lambda i, j: (0, 0)),
                  pl.BlockSpec((tn, D), lambda i, j: (src_block(j), 0))],
        out_specs=pl.BlockSpec((tm, tn), lambda i, j: (i, j)),
        out_shape=jax.ShapeDtypeStruct((M, IN_PAD), F32),
        scratch_shapes=[pltpu.VMEM((tm, D), BF16)],
        compiler_params=_params("parallel", "arbitrary"),
        name="inproj",
    )(x, norm_g.reshape(1, D), w_t)


def _outproj_kernel(x_ref, oa_ref, ob_ref, oat_ref, obt_ref, w_ref, o_ref):
    half = oa_ref.shape[1]
    own = x_ref.shape[0] - oat_ref.shape[0]
    last_tile = pl.program_id(0) == pl.num_programs(0) - 1

    def run(oa, ob):
        o_ref[...] = (x_ref[...]
                      + jnp.dot(oa.astype(BF16), w_ref[:half, :], preferred_element_type=F32)
                      + jnp.dot(ob.astype(BF16), w_ref[half:, :], preferred_element_type=F32))

    pl.when(jnp.logical_not(last_tile))(lambda: run(oa_ref[...], ob_ref[...]))
    pl.when(last_tile)(lambda: run(jnp.concatenate([oa_ref[:own, :], oat_ref[...]], axis=0),
                                   jnp.concatenate([ob_ref[:own, :], obt_ref[...]], axis=0)))


def _outproj(x, o_rwkv, o_attn, o_rwkv_tail, o_attn_tail, w, *, tm):
    M, D = x.shape
    Wd = o_rwkv.shape[1]
    tail = pl.BlockSpec(o_rwkv_tail.shape, lambda i: (0, 0))
    return pl.pallas_call(
        _outproj_kernel,
        grid=(M // tm,),
        in_specs=[pl.BlockSpec((tm, D), lambda i: (i, 0)),
                  pl.BlockSpec((tm, Wd), lambda i: (i, 0)),
                  pl.BlockSpec((tm, Wd), lambda i: (i, 0)),
                  tail, tail,
                  pl.BlockSpec(w.shape, lambda i: (0, 0))],
        out_specs=pl.BlockSpec((tm, D), lambda i: (i, 0)),
        out_shape=jax.ShapeDtypeStruct((M, D), F32),
        compiler_params=_params("arbitrary"),
        name="outproj",
    )(x, o_rwkv, o_attn, o_rwkv_tail, o_attn_tail, w)


def _head_ones():
    r = lax.broadcasted_iota(jnp.int32, (LANES, LANES), 0) // HEAD_DIM
    c = lax.broadcasted_iota(jnp.int32, (LANES, LANES), 1) // HEAD_DIM
    return (r == c).astype(F32)


def _head_sum(x):
    ones = _head_ones().astype(BF16)
    terms = _split_terms(x, 2)
    parts = [sum(jnp.dot(t[:, j * LANES:(j + 1) * LANES], ones, preferred_element_type=F32) for t in terms)
             for j in range(x.shape[1] // LANES)]
    return parts[0] if len(parts) == 1 else jnp.concatenate(parts, axis=-1)


def _rwkv_prep_math(p_r, p_k, p_v, p_l, q_r, q_k, q_v, q_l, w):
    (mu_r, mu_k, mu_v, mu_l, dec_up, dec_base, icl_up, icl_base, gate_up, k_k, k_a) = w
    r = p_r + mu_r * (q_r - p_r)
    k = p_k + mu_k * (q_k - p_k)
    v = p_v + mu_v * (q_v - p_v)
    ul = p_l + mu_l * (q_l - p_l)
    w_log = -jax.nn.softplus(-(dec_base + _dot(jnp.tanh(ul), dec_up))) - 0.5
    lw = -jnp.exp(w_log)
    a = jax.nn.sigmoid(icl_base + _dot(ul, icl_up))
    g = _dot(jax.nn.sigmoid(ul), gate_up)
    kk = k * k_k
    kk = kk / jnp.maximum(jnp.sqrt(_head_sum(kk * kk)), 1e-12)
    k2 = k * (1.0 + (a - 1.0) * k_a)
    return r, lw, k2, v, kk, kk * a, g


def _prep_sample_kernel(pr, pk, pv, pL, qr, qk, qv, qL, *rest):
    w_refs, out_refs = rest[:11], rest[11:]
    outs = _rwkv_prep_math(pr[...], pk[...], pv[...], pL[...], qr[...], qk[...], qv[...], qL[...],
                           tuple(r[...] for r in w_refs))
    for o_ref, o in zip(out_refs, outs):
        o_ref[...] = o


def _prep_weight_specs(w, nargs):
    zero = (lambda b, i: (0, 0)) if nargs == 2 else (lambda i: (0, 0))
    return [pl.BlockSpec(a.shape, zero) for a in w]


def _prep_sample(p_all, shift_r, shift_k, shift_v, shift_l, w, *, row0, rows):
    W = RWKV_WIDTH
    rb = row0 // rows

    def cur(col_block, width):
        return pl.BlockSpec((rows, width), lambda i: (rb, col_block))

    def full(width):
        return pl.BlockSpec((rows, width), lambda i: (0, 0))

    in_specs = [cur(0, W), cur(1, W), cur(2, W), cur(COL_LORA // LORA_PAD, LORA_PAD),
                full(W), full(W), full(W), full(LORA_PAD)]
    in_specs += _prep_weight_specs(w, 1)
    out = jax.ShapeDtypeStruct((rows, W), F32)
    return pl.pallas_call(
        _prep_sample_kernel,
        grid=(1,),
        in_specs=in_specs,
        out_specs=[pl.BlockSpec((rows, W), lambda i: (0, 0))] * 7,
        out_shape=[out] * 7,
        compiler_params=_params("arbitrary"),
        name="rwkv_prep_sample",
    )(p_all, p_all, p_all, p_all, shift_r, shift_k, shift_v, shift_l, *w)


def _stack_heads(x):
    lane = lax.broadcasted_iota(jnp.int32, x.shape, 1)
    return jnp.concatenate([jnp.where(lane < HEAD_DIM, x, 0.0), jnp.where(lane >= HEAD_DIM, x, 0.0)], axis=0)


def _split_terms(x, n):
    terms = []
    for _ in range(n):
        hi = x.astype(BF16)
        terms.append(hi)
        x = x - hi.astype(F32)
    return terms


def _rwkv_chunk_pair(rg, nag, bi, ki, bend, kend, v, etot, st, consts):
    C = CHUNK
    strict, incl, eye = consts
    rg, nag, bi, ki, bend, kend, vs = (_stack_heads(a) for a in (rg, nag, bi, ki, bend, kend, v))

    gram = _dot_nt(jnp.concatenate([nag, rg], axis=0), jnp.concatenate([bi, ki], axis=0))
    yield
    a_ab = jnp.where(strict, gram[:2 * C, :2 * C], 0.0)
    a_ak = jnp.where(strict, gram[:2 * C, 2 * C:], 0.0)
    a_rb = jnp.where(incl, gram[2 * C:, :2 * C], 0.0)
    a_rk = jnp.where(incl, gram[2 * C:, 2 * C:], 0.0)

    z = _dot_nt(nag, st) + _dot(a_ak, vs)
    o2 = _dot_nt(rg, st) + _dot(a_rk, vs)
    yield

    pw = a_ab
    inv = eye + pw
    span = 1
    while span * 2 < C:
        pw = _dot(pw, pw)
        yield
        inv = inv + _dot(inv, pw)
        span *= 2
    yield

    u = _dot(inv, z)
    yield
    o2 = o2 + _dot(a_rb, u)
    st_new = st * etot + _dot(u.T, bend) + _dot(vs.T, kend)
    yield o2[:C, :] + o2[C:, :], st_new


def _rwkv_scan_kernel(pr, pk, pv, pL, qr, qk, qv, qL, *rest):
    w_refs = rest[:11]
    rk_ref, lng_ref, lnb_ref, o_ref, s_out_ref = rest[11:16]
    chunk_refs = rest[16:24]
    bonus_ref, gate_ref, st_ref = rest[24:27]
    C = CHUNK
    n_chunks = pr.shape[0] // C
    pairs = pr.shape[1] // LANES
    first = pl.program_id(1) == 0

    @pl.when(first)
    def _():
        st_ref[...] = jnp.zeros_like(st_ref)

    def prev_rows(cur_ref, tail_ref):
        cur = cur_ref[...]
        tail = jnp.where(first, 0.0, tail_ref[7:8, :])
        row0 = lax.broadcasted_iota(jnp.int32, cur.shape, 0) == 0
        return jnp.where(row0, tail, pltpu.roll(cur, 1, axis=0))

    r, lw, k, v, kk, b, g = _rwkv_prep_math(
        pr[...], pk[...], pv[...], pL[...],
        prev_rows(pr, qr), prev_rows(pk, qk), prev_rows(pv, qv), prev_rows(pL, qL),
        tuple(ref[...] for ref in w_refs))

    tb = pr.shape[0]
    ti = lax.broadcasted_iota(jnp.int32, (tb, tb), 0)
    tj = lax.broadcasted_iota(jnp.int32, (tb, tb), 1)
    same_chunk = (ti // C) == (tj // C)
    lw_terms = _split_terms(lw, 3)
    tri = (same_chunk & (tj <= ti)).astype(BF16)
    cum = sum(jnp.dot(tri, t, preferred_element_type=F32) for t in lw_terms)
    tot = sum(jnp.dot(same_chunk.astype(BF16), t, preferred_element_type=F32) for t in lw_terms)
    ginv = jnp.exp(-cum)
    gend = jnp.exp(tot - cum)
    scaled = (r * jnp.exp(cum), -kk * jnp.exp(cum - lw), b * ginv, k * ginv, b * gend, k * gend, v, jnp.exp(tot))
    for ref, val in zip(chunk_refs, scaled):
        ref[...] = val
    bonus_ref[...] = _head_sum(r * k * rk_ref[...]) * v
    gate_ref[...] = g

    row = lax.broadcasted_iota(jnp.int32, (2 * C, 2 * C), 0)
    col = lax.broadcasted_iota(jnp.int32, (2 * C, 2 * C), 1)
    consts = (col < row, col <= row, (col == row).astype(F32))

    def chunk(c, carry):
        rows = pl.ds(pl.multiple_of(c * C, C), C)
        def pair(j):
            lanes = slice(j * LANES, (j + 1) * LANES)
            tok = [ref[rows, lanes] for ref in chunk_refs]
            tok[-1] = tok[-1][:1, :]
            return _rwkv_chunk_pair(*tok, st_ref[j], consts)

        stages = [pair(j) for j in range(pairs)]
        results = [None] * pairs
        while results[-1] is None:
            for j, gen in enumerate(stages):
                results[j] = next(gen)
        for j, (o, st_new) in enumerate(results):
            o_ref[rows, j * LANES:(j + 1) * LANES] = o
            st_ref[j] = st_new
        return carry

    lax.fori_loop(0, n_chunks, chunk, 0)

    o = o_ref[...]
    mean = _head_sum(o) * (1.0 / HEAD_DIM)
    d = o - mean
    var = _head_sum(d * d) * (1.0 / HEAD_DIM)
    o_ref[...] = (d * lax.rsqrt(var + GN_EPS) * lng_ref[...] + lnb_ref[...] + bonus_ref[...]) * gate_ref[...]

    @pl.when(pl.program_id(1) == pl.num_programs(1) - 1)
    def _():
        for j in range(pairs):
            st = st_ref[j]
            s_out_ref[0, 2 * j] = st[:HEAD_DIM, :HEAD_DIM]
            s_out_ref[0, 2 * j + 1] = st[HEAD_DIM:, HEAD_DIM:]


def _rwkv_scan(p_all, prep_w, r_k, lnx_g, lnx_b, *, batch, seq, tb):
    nt = seq // tb
    W = RWKV_WIDTH
    pairs = W // LANES

    def cur(col_block, width):
        return pl.BlockSpec((tb, width), lambda b, i: (b * nt + i, col_block))

    def tail(col_block, width):
        return pl.BlockSpec((8, width), lambda b, i: (jnp.maximum((b * seq + i * tb) // 8 - 1, 0), col_block))

    lora_block = COL_LORA // LORA_PAD
    in_specs = [cur(0, W), cur(1, W), cur(2, W), cur(lora_block, LORA_PAD),
                tail(0, W), tail(1, W), tail(2, W), tail(lora_block, LORA_PAD)]
    in_specs += _prep_weight_specs(prep_w, 2)
    in_specs += [pl.BlockSpec((1, W), lambda b, i: (0, 0))] * 3
    return pl.pallas_call(
        _rwkv_scan_kernel,
        grid=(batch, nt),
        in_specs=in_specs,
        out_specs=[pl.BlockSpec((tb, W), lambda b, i: (b * nt + i, 0)),
                   pl.BlockSpec((1, N_HEADS, HEAD_DIM, HEAD_DIM), lambda b, i: (b, 0, 0, 0))],
        out_shape=[jax.ShapeDtypeStruct((batch * seq, W), F32),
                   jax.ShapeDtypeStruct((batch, N_HEADS, HEAD_DIM, HEAD_DIM), F32)],
        scratch_shapes=[pltpu.VMEM((tb, W), F32)] * 10 + [pltpu.VMEM((pairs, LANES, LANES), F32)],
        compiler_params=_params("parallel", "arbitrary"),
        name="rwkv_scan",
    )(*([p_all] * 8), *prep_w, r_k, lnx_g, lnx_b)


def _rwkv_step_kernel(s_ref, r_ref, lw_ref, k_ref, v_ref, kk_ref, b_ref, g_ref, rk_ref, lng_ref, lnb_ref,
                      o_ref, s_out_ref):
    s = s_ref[...]
    r, k, v, kk, b, g = (ref[...] for ref in (r_ref, k_ref, v_ref, kk_ref, b_ref, g_ref))
    w = jnp.exp(lw_ref[...])
    shape = (HEAD_DIM, HEAD_DIM)
    eye = lax.broadcasted_iota(jnp.int32, shape, 0) == lax.broadcasted_iota(jnp.int32, shape, 1)
    sa = jnp.sum(s * (-kk), axis=-1, keepdims=True)
    v_col = jnp.sum(jnp.where(eye, v, 0.0), axis=-1, keepdims=True)
    s_new = s * w + sa * b + v_col * k
    s_out_ref[...] = s_new
    o_col = jnp.sum(s_new * r, axis=-1, keepdims=True)
    o = jnp.sum(jnp.where(eye, o_col, 0.0), axis=-2, keepdims=True)
    mean = jnp.mean(o, axis=-1, keepdims=True)
    d = o - mean
    var = jnp.mean(d * d, axis=-1, keepdims=True)
    on = d * lax.rsqrt(var + GN_EPS) * lng_ref[...] + lnb_ref[...]
    bonus = jnp.sum(r * k * rk_ref[...], axis=-1, keepdims=True) * v
    o_ref[...] = (on + bonus) * g


def _rwkv_step(state, r, lw, k, v, kk, b, g, r_k, lnx_g, lnx_b, *, bt):
    B = state.shape[0]
    vec4 = lambda a: a.reshape(B, N_HEADS, 1, HEAD_DIM)
    par4 = lambda a: a.reshape(1, N_HEADS, 1, HEAD_DIM)
    st_spec = pl.BlockSpec((bt, N_HEADS, HEAD_DIM, HEAD_DIM), lambda i: (i, 0, 0, 0))
    vec_spec = pl.BlockSpec((bt, N_HEADS, 1, HEAD_DIM), lambda i: (i, 0, 0, 0))
    par_spec = pl.BlockSpec((1, N_HEADS, 1, HEAD_DIM), lambda i: (0, 0, 0, 0))
    o, s_new = pl.pallas_call(
        _rwkv_step_kernel,
        grid=(B // bt,),
        in_specs=[st_spec] + [vec_spec] * 7 + [par_spec] * 3,
        out_specs=[vec_spec, st_spec],
        out_shape=[jax.ShapeDtypeStruct((B, N_HEADS, 1, HEAD_DIM), F32),
                   jax.ShapeDtypeStruct(state.shape, F32)],
        compiler_params=_params("parallel"),
        name="rwkv_step",
    )(state, *(vec4(a) for a in (r, lw, k, v, kk, b, g)), par4(r_k), par4(lnx_g), par4(lnx_b))
    return o.reshape(B, RWKV_WIDTH), s_new


def _swa_prompt_kernel(q_ref, kc_ref, kp_ref, vc_ref, vp_ref, slope_ref, sink_ref, o_ref):
    blk = q_ref.shape[0]
    has_prev = pl.program_id(1) > 0
    kj = lax.broadcasted_iota(jnp.int32, (2 * blk, blk), 0)
    qi = lax.broadcasted_iota(jnp.int32, (2 * blk, blk), 1) + blk
    dist = qi - kj
    valid = (dist >= 0) & (dist < WINDOW) & (has_prev | (kj >= blk))
    distf = dist.astype(F32)

    low_lane = lax.broadcasted_iota(jnp.int32, (2 * blk, LANES), 1) < HEAD_DIM
    low_row = lax.broadcasted_iota(jnp.int32, (LANES, 2 * blk), 0) < HEAD_DIM
    kv = []
    for grp in range(kc_ref.shape[1] // LANES):
        lanes = slice(grp * LANES, (grp + 1) * LANES)
        kcat = jnp.concatenate([kp_ref[:, lanes], kc_ref[:, lanes]], axis=0)
        kroll = pltpu.roll(kcat, HEAD_DIM, axis=1)
        vt = jnp.concatenate([vp_ref[:, lanes], vc_ref[:, lanes]], axis=0).T
        vroll = pltpu.roll(vt, HEAD_DIM, axis=0)
        kv += [(jnp.where(low_lane, kcat, 0.0).astype(BF16), jnp.where(low_lane, 0.0, kroll).astype(BF16),
                jnp.where(low_row, vt, 0.0).astype(BF16), jnp.where(low_row, 0.0, vroll).astype(BF16)),
               (jnp.where(low_lane, kroll, 0.0).astype(BF16), jnp.where(low_lane, 0.0, kcat).astype(BF16),
                jnp.where(low_row, vroll, 0.0).astype(BF16), jnp.where(low_row, 0.0, vt).astype(BF16))]
    out_low_row = lax.broadcasted_iota(jnp.int32, (LANES, blk), 0) < HEAD_DIM

    def softmax_t(s, h):
        s = jnp.where(valid, s - slope_ref[h][:, :blk] * distf, NEG_INF)
        sink = sink_ref[h][:, :blk]
        m = jnp.maximum(jnp.max(s, axis=0, keepdims=True), sink)
        e = jnp.exp(s - m)
        return e, jnp.sum(e, axis=0, keepdims=True) + jnp.exp(sink - m)

    def head_pair(i):
        k_lo, k_hi, v_lo, v_hi = kv[i // 2]
        cols = slice(i * LANES, (i + 1) * LANES)
        q = (q_ref[:, cols] * (HEAD_DIM ** -0.5)).astype(BF16)
        s_lo = lax.dot_general(k_lo, q, (((1,), (1,)), ((), ())), preferred_element_type=F32)
        s_hi = lax.dot_general(k_hi, q, (((1,), (1,)), ((), ())), preferred_element_type=F32)
        yield
        e_lo, d_lo = softmax_t(s_lo, 2 * i)
        e_hi, d_hi = softmax_t(s_hi, 2 * i + 1)
        ot = (jnp.dot(v_lo, e_lo.astype(BF16), preferred_element_type=F32)
              + jnp.dot(v_hi, e_hi.astype(BF16), preferred_element_type=F32))
        yield
        o_ref[:, cols] = (ot / jnp.where(out_low_row, d_lo, d_hi)).T
        yield

    pairs = [head_pair(i) for i in range(q_ref.shape[1] // LANES)]
    for _ in range(3):
        for gen in pairs:
            next(gen)


def _swa_prompt(p_all, slopes, sinks, *, batch, seq):
    blk = WINDOW
    nb = seq // blk
    cur = lambda col: (lambda b, i: (b * nb + i, col))
    prev = lambda col: (lambda b, i: (b * nb + jnp.maximum(i - 1, 0), col))
    par = pl.BlockSpec((N_HEADS, 1, 2 * blk), lambda b, i: (0, 0, 0))
    return pl.pallas_call(
        _swa_prompt_kernel,
        grid=(batch, nb),
        in_specs=[pl.BlockSpec((blk, ATTN_WIDTH), cur(COL_Q // ATTN_WIDTH)),
                  pl.BlockSpec((blk, KV_WIDTH), cur(COL_AK // KV_WIDTH)),
                  pl.BlockSpec((blk, KV_WIDTH), prev(COL_AK // KV_WIDTH)),
                  pl.BlockSpec((blk, KV_WIDTH), cur(COL_AV // KV_WIDTH)),
                  pl.BlockSpec((blk, KV_WIDTH), prev(COL_AV // KV_WIDTH)),
                  par, par],
        out_specs=pl.BlockSpec((blk, ATTN_WIDTH), lambda b, i: (b * nb + i, 0)),
        out_shape=jax.ShapeDtypeStruct((batch * seq, ATTN_WIDTH), F32),
        compiler_params=_params("parallel", "arbitrary"),
        name="swa_prompt",
    )(p_all, p_all, p_all, p_all, p_all,
      jnp.broadcast_to(slopes.reshape(N_HEADS, 1, 1), (N_HEADS, 1, 2 * blk)),
      jnp.broadcast_to(sinks.reshape(N_HEADS, 1, 1), (N_HEADS, 1, 2 * blk)))


def _swa_sample_kernel(q_ref, kn_ref, vn_ref, ck_ref, cv_ref, slope_ref, sink_ref, o_ref, kw_ref, vw_ref):
    bt = q_ref.shape[0]
    win = ck_ref.shape[1]
    row = lax.broadcasted_iota(jnp.int32, (win, KV_WIDTH), 0)
    head_kv = lax.broadcasted_iota(jnp.int32, (N_HEADS, KV_WIDTH), 0) // GQA_GROUP
    lane_kv = lax.broadcasted_iota(jnp.int32, (N_HEADS, KV_WIDTH), 1) // HEAD_DIM
    own = head_kv == lane_kv
    dist = (win - 1 - lax.broadcasted_iota(jnp.int32, (N_HEADS, win), 1)).astype(F32)
    bias = slope_ref[...] * dist
    sink = sink_ref[...][:, :1]
    for i in range(bt):
        keys = jnp.where(row == win - 1, kn_ref[i:i + 1, :], pltpu.roll(ck_ref[i], win - 1, axis=0))
        vals = jnp.where(row == win - 1, vn_ref[i:i + 1, :], pltpu.roll(cv_ref[i], win - 1, axis=0))
        kw_ref[i] = keys
        vw_ref[i] = vals
        q = q_ref[i]
        qbd = jnp.where(own, jnp.concatenate([q] * N_KV_HEADS, axis=-1), 0.0)
        s = _dot_nt(qbd, keys) * (HEAD_DIM ** -0.5) - bias
        m = jnp.maximum(jnp.max(s, axis=-1, keepdims=True), sink)
        e = jnp.exp(s - m)
        denom = jnp.sum(e, axis=-1, keepdims=True) + jnp.exp(sink - m)
        o2 = jnp.where(own, _dot(e, vals), 0.0)
        o = o2[:, :HEAD_DIM]
        for c in range(1, N_KV_HEADS):
            o = o + o2[:, c * HEAD_DIM:(c + 1) * HEAD_DIM]
        o_ref[i] = o / denom


def _swa_sample(q, k_new, v_new, cache_k, cache_v, slopes, sinks, *, bt):
    B, win = cache_k.shape[0], cache_k.shape[1]
    q3 = pl.BlockSpec((bt, N_HEADS, HEAD_DIM), lambda i: (i, 0, 0))
    new = pl.BlockSpec((bt, KV_WIDTH), lambda i: (i, 0))
    cache = pl.BlockSpec((bt, win, KV_WIDTH), lambda i: (i, 0, 0))
    par = pl.BlockSpec((N_HEADS, win), lambda i: (0, 0))
    o, kw, vw = pl.pallas_call(
        _swa_sample_kernel,
        grid=(B // bt,),
        in_specs=[q3, new, new, cache, cache, par, par],
        out_specs=[q3, cache, cache],
        out_shape=[jax.ShapeDtypeStruct((B, N_HEADS, HEAD_DIM), F32),
                   jax.ShapeDtypeStruct((B, win, KV_WIDTH), F32),
                   jax.ShapeDtypeStruct((B, win, KV_WIDTH), F32)],
        compiler_params=_params("parallel"),
        name="swa_sample",
    )(q.reshape(B, N_HEADS, HEAD_DIM), k_new, v_new,
      cache_k.reshape(B, win, KV_WIDTH), cache_v.reshape(B, win, KV_WIDTH),
      jnp.broadcast_to(slopes.reshape(N_HEADS, 1), (N_HEADS, win)),
      jnp.broadcast_to(sinks.reshape(N_HEADS, 1), (N_HEADS, win)))
    return o.reshape(B, ATTN_WIDTH), kw, vw


def _permute_in_cols(a):
    attn_proj = ATTN_WIDTH + 2 * KV_WIDTH
    rkv = a[..., attn_proj:attn_proj + 3 * RWKV_WIDTH]
    lora = a[..., attn_proj + 3 * RWKV_WIDTH:]
    pad = jnp.zeros(a.shape[:-1] + (LORA_PAD - LORA_WIDTH,), a.dtype)
    return jnp.concatenate([rkv, a[..., :attn_proj], lora, pad], axis=-1)


def _lora_rows(w, start):
    return jnp.zeros((LORA_PAD, w.shape[1]), F32).at[start:start + w.shape[0]].set(w).astype(BF16)


def kernel(x_prompt, x_sample, state_rwkv_shift, state_rwkv_wkv, cache_swa_k, cache_swa_v, ffn1_norm, ffn1_w_gate, ffn1_w_up, ffn1_w_down, mix_norm, w_in, rwkv_mu, rwkv_decay_up, rwkv_decay_base, rwkv_iclr_up, rwkv_iclr_base, rwkv_gate_up, rwkv_k_k, rwkv_k_a, rwkv_r_k, rwkv_lnx_g, rwkv_lnx_b, attn_sinks, w_out, ffn2_norm, ffn2_w_gate, ffn2_w_up, ffn2_w_down, final_norm):
    batch, seq, d_model = x_prompt.shape
    dec_batch = x_sample.shape[0]
    depth = ffn1_norm.shape[0]
    n_prompt = batch * seq
    win = cache_swa_k.shape[2]
    assert x_sample.shape[1] == 1 and seq % 512 == 0 and win == WINDOW and n_prompt % dec_batch == 0
    tm, tm_ffn, tf, tm_in = 640, 832, 512, 1664
    assert all((n_prompt + dec_batch) % t == 0 for t in (tm, tm_ffn, tm_in))

    x = x_prompt.reshape(n_prompt, d_model)
    slopes = jnp.exp2(-8.0 * jnp.arange(1, N_HEADS + 1, dtype=F32) / N_HEADS)
    outs = [[] for _ in range(8)]
    for l in range(depth):
        x = _ffn(x, ffn1_norm[l], ffn1_w_gate[l], ffn1_w_up[l], ffn1_w_down[l], tm=tm_ffn, tf=tf,
                 x_tail=x_sample.reshape(dec_batch, d_model) if l == 0 else None)
        p_all = _inproj(x, mix_norm[l], w_in[l].T, tm=tm_in)

        mu = _permute_in_cols(jnp.concatenate([jnp.zeros((ATTN_WIDTH + 2 * KV_WIDTH,), F32), rwkv_mu[l]]))
        row = lambda a: a.reshape(1, -1)
        prep_w = (row(mu[COL_R:COL_K]), row(mu[COL_K:COL_V]), row(mu[COL_V:COL_Q]), row(mu[COL_LORA:]),
                  _lora_rows(rwkv_decay_up[l], 0), row(rwkv_decay_base[l]),
                  _lora_rows(rwkv_iclr_up[l], 64), row(rwkv_iclr_base[l]),
                  _lora_rows(rwkv_gate_up[l], 128), row(rwkv_k_k[l]), row(rwkv_k_a[l]))
        head_par = (row(rwkv_r_k[l]), row(rwkv_lnx_g[l]), row(rwkv_lnx_b[l]))

        o_rwkv_p, wkv_p = _rwkv_scan(p_all, prep_w, *head_par, batch=batch, seq=seq, tb=256)
        o_attn_p = _swa_prompt(p_all, slopes, attn_sinks[l], batch=batch, seq=seq)

        sh = _permute_in_cols(jnp.concatenate(
            [jnp.zeros((dec_batch, ATTN_WIDTH + 2 * KV_WIDTH), F32), state_rwkv_shift[l]], axis=-1))
        prep_s = _prep_sample(p_all, sh[:, COL_R:COL_K], sh[:, COL_K:COL_V], sh[:, COL_V:COL_Q], sh[:, COL_LORA:],
                              prep_w, row0=n_prompt, rows=dec_batch)
        o_rwkv_s, wkv_s = _rwkv_step(state_rwkv_wkv[l], *prep_s, *head_par, bt=8)
        p_s = p_all[n_prompt:]
        o_attn_s, kwin_s, vwin_s = _swa_sample(p_s[:, COL_Q:COL_AK], p_s[:, COL_AK:COL_AV], p_s[:, COL_AV:COL_LORA],
                                               cache_swa_k[l], cache_swa_v[l], slopes, attn_sinks[l], bt=8)

        x = _outproj(x, o_rwkv_p, o_attn_p, o_rwkv_s, o_attn_s, w_out[l].astype(BF16), tm=tm)
        last_layer = l == depth - 1
        x = _ffn(x, ffn2_norm[l], ffn2_w_gate[l], ffn2_w_up[l], ffn2_w_down[l], final_norm if last_layer else None,
                 tm=tm_ffn, tf=tf, split_tail=dec_batch if last_layer else 0)

        wb = min(WINDOW, seq)
        tails = [p_all[(b + 1) * seq - wb:(b + 1) * seq] for b in range(batch)]
        last = jnp.concatenate([t[wb - 1:] for t in tails], axis=0)
        shift_cols = lambda a: jnp.concatenate([a[:, COL_R:COL_Q], a[:, COL_LORA:COL_LORA + LORA_WIDTH]], axis=-1)
        for lst, val in zip(outs, (
                shift_cols(last), wkv_p,
                jnp.stack([t[:, COL_AK:COL_AV] for t in tails]).reshape(batch, wb, N_KV_HEADS, HEAD_DIM),
                jnp.stack([t[:, COL_AV:COL_LORA] for t in tails]).reshape(batch, wb, N_KV_HEADS, HEAD_DIM),
                shift_cols(p_s), wkv_s,
                kwin_s.reshape(dec_batch, win, N_KV_HEADS, HEAD_DIM),
                vwin_s.reshape(dec_batch, win, N_KV_HEADS, HEAD_DIM))):
            lst.append(val)

    y_prompt, y_sample = x
    return ((y_prompt.reshape(batch, seq, d_model), y_sample.reshape(dec_batch, 1, d_model))
            + tuple(jnp.stack(o) for o in outs))
```

```python
import functools

import jax
import jax.numpy as jnp
from jax import lax
from jax.experimental import pallas as pl
from jax.experimental.pallas import tpu as pltpu

F32 = jnp.float32
BF16 = jnp.bfloat16
HIGHEST = lax.Precision.HIGHEST

HEAD_DIM = 64
N_HEADS = 16
N_KV_HEADS = 4
GQA_GROUP = 4
RWKV_WIDTH = 1024
ATTN_WIDTH = 1024
KV_WIDTH = 256
WINDOW = 128
LORA_WIDTH = 64 + 64 + 160
LORA_PAD = 384
RMS_EPS = 1e-6
GN_EPS = 64e-5
NEG_INF = -1e30

COL_R, COL_K, COL_V = 0, 1024, 2048
COL_Q = 3072
COL_AK = 4096
COL_AV = 4352
COL_LORA = 4608
IN_BLOCK = 512
IN_PAD = COL_LORA + IN_BLOCK

LANES = 128
CHUNK = 64
VMEM_LIMIT = 60 * 1024 * 1024


def _dot(a, b):
    return jnp.dot(a.astype(BF16), b.astype(BF16), preferred_element_type=F32)


def _dot_nt(a, b):
    return lax.dot_general(a.astype(BF16), b.astype(BF16), (((1,), (1,)), ((), ())),
                           preferred_element_type=F32)


def _dot_hi(a, b):
    return jnp.dot(a, b, preferred_element_type=F32, precision=HIGHEST)


def _rms(x, g):
    return x * lax.rsqrt(jnp.mean(x * x, axis=-1, keepdims=True) + RMS_EPS) * g


def _params(*sem):
    return pltpu.CompilerParams(dimension_semantics=sem, vmem_limit_bytes=VMEM_LIMIT)


def _ffn_kernel(*refs, final, d_ff, tail_in, tail_out):
    refs = list(refs)
    x_ref = refs.pop(0)
    xt_ref = refs.pop(0) if tail_in else None
    g_ref, wg_ref, wu_ref, wd_ref = (refs.pop(0) for _ in range(4))
    fg_ref = refs.pop(0) if final else None
    o_ref = refs.pop(0)
    ot_ref = refs.pop(0) if tail_out else None
    (xn_ref,) = refs
    f = pl.program_id(1)
    last_tile = pl.program_id(0) == pl.num_programs(0) - 1
    tm = x_ref.shape[0]
    tf = wg_ref.shape[1]
    nf = -(-d_ff // tf)
    last_width = d_ff - (nf - 1) * tf

    def start(x):
        xn_ref[...] = _rms(x, g_ref[...]).astype(BF16)
        o_ref[...] = x

    if tail_in:
        own = tm - xt_ref.shape[0]
        pl.when((f == 0) & jnp.logical_not(last_tile))(lambda: start(x_ref[...]))
        pl.when((f == 0) & last_tile)(lambda: start(jnp.concatenate([x_ref[:own, :], xt_ref[...]], axis=0)))
    else:
        pl.when(f == 0)(lambda: start(x_ref[...]))

    def accumulate(width):
        xn = xn_ref[...]
        hg = jnp.dot(xn, wg_ref[:, :width].astype(BF16), preferred_element_type=F32)
        hu = jnp.dot(xn, wu_ref[:, :width].astype(BF16), preferred_element_type=F32)
        h = (hg * jax.nn.sigmoid(hg) * (0.5 * hu)).astype(BF16)
        o_ref[...] += jnp.dot(h, wd_ref[:width, :].astype(BF16), preferred_element_type=F32)

    if last_width == tf:
        accumulate(tf)
    else:
        pl.when(f < nf - 1)(lambda: accumulate(tf))
        pl.when(f == nf - 1)(lambda: accumulate(last_width))

    if final:
        @pl.when(f == nf - 1)
        def _():
            o_ref[...] = _rms(o_ref[...], fg_ref[...])

    if tail_out:
        @pl.when((f == nf - 1) & last_tile)
        def _():
            ot_ref[...] = o_ref[tm - ot_ref.shape[0]:, :]


def _ffn(x, norm_g, wg, wu, wd, final_g=None, *, tm, tf, x_tail=None, split_tail=0):
    D = x.shape[1]
    M = x.shape[0] + (0 if x_tail is None else x_tail.shape[0])
    d_ff = wg.shape[1]
    final = final_g is not None
    const = lambda i, f: (0, 0)
    in_specs = [pl.BlockSpec((tm, D), lambda i, f: (i, 0))]
    args = [x]
    if x_tail is not None:
        in_specs.append(pl.BlockSpec(x_tail.shape, const))
        args.append(x_tail)
    in_specs += [pl.BlockSpec((1, D), const),
                 pl.BlockSpec((D, tf), lambda i, f: (0, f)),
                 pl.BlockSpec((D, tf), lambda i, f: (0, f)),
                 pl.BlockSpec((tf, D), lambda i, f: (f, 0))]
    args += [norm_g.reshape(1, D), wg, wu, wd]
    if final:
        in_specs.append(pl.BlockSpec((1, D), const))
        args.append(final_g.reshape(1, D))
    out_specs = [pl.BlockSpec((tm, D), lambda i, f: (i, 0))]
    out_shape = [jax.ShapeDtypeStruct((M - split_tail, D), F32)]
    if split_tail:
        out_specs.append(pl.BlockSpec((split_tail, D), const))
        out_shape.append(jax.ShapeDtypeStruct((split_tail, D), F32))
    out = pl.pallas_call(
        functools.partial(_ffn_kernel, final=final, d_ff=d_ff, tail_in=x_tail is not None,
                          tail_out=bool(split_tail)),
        grid=(M // tm, pl.cdiv(d_ff, tf)),
        in_specs=in_specs,
        out_specs=out_specs,
        out_shape=out_shape,
        scratch_shapes=[pltpu.VMEM((tm, D), BF16)],
        compiler_params=_params("arbitrary", "arbitrary"),
        name="ffn",
    )(*args)
    return out if split_tail else out[0]


def _inproj_kernel(x_ref, g_ref, w_ref, o_ref, xn_ref, *, last_valid):
    j = pl.program_id(1)
    nj = pl.num_programs(1)

    @pl.when(j == 0)
    def _():
        xn_ref[...] = _rms(x_ref[...], g_ref[...]).astype(BF16)

    def run(wt):
        o_ref[...] = lax.dot_general(xn_ref[...], wt.astype(BF16), (((1,), (1,)), ((), ())),
                                     preferred_element_type=F32)

    row = lax.broadcasted_iota(jnp.int32, w_ref.shape, 0)
    pl.when(j < nj - 1)(lambda: run(w_ref[...]))
    pl.when(j == nj - 1)(lambda: run(jnp.where(row < last_valid, w_ref[...], 0.0)))


def _inproj(x, norm_g, w_t, *, tm):
    M, D = x.shape
    n_src = w_t.shape[0]
    tn = IN_BLOCK
    nj = IN_PAD // tn
    assert pl.cdiv(n_src, tn) == nj and COL_LORA == (nj - 1) * tn

    def src_block(j):
        return jnp.where(j < 6, j + 3, jnp.where(j < 9, j - 6, j))

    return pl.pallas_call(
        functools.partial(_inproj_kernel, last_valid=n_src - COL_LORA),
        grid=(M // tm, nj),
        in_specs=[pl.BlockSpec((tm, D), lambda i, j: (i, 0)),
                  pl.BlockSpec((1, D), lambda i, j: (0, 0)),
                  pl.BlockSpec((tn, D), lambda i, j: (src_block(j), 0))],
        out_specs=pl.BlockSpec((tm, tn), lambda i, j: (i, j)),
        out_shape=jax.ShapeDtypeStruct((M, IN_PAD), F32),
        scratch_shapes=[pltpu.VMEM((tm, D), BF16)],
        compiler_params=_params("parallel", "arbitrary"),
        name="inproj",
    )(x, norm_g.reshape(1, D), w_t)


def _outproj_kernel(x_ref, oa_ref, ob_ref, oat_ref, obt_ref, w_ref, o_ref):
    half = oa_ref.shape[1]
    own = x_ref.shape[0] - oat_ref.shape[0]
    last_tile = pl.program_id(0) == pl.num_programs(0) - 1

    def run(oa, ob):
        o_ref[...] = (x_ref[...]
                      + jnp.dot(oa.astype(BF16), w_ref[:half, :], preferred_element_type=F32)
                      + jnp.dot(ob.astype(BF16), w_ref[half:, :], preferred_element_type=F32))

    pl.when(jnp.logical_not(last_tile))(lambda: run(oa_ref[...], ob_ref[...]))
    pl.when(last_tile)(lambda: run(jnp.concatenate([oa_ref[:own, :], oat_ref[...]], axis=0),
                                   jnp.concatenate([ob_ref[:own, :], obt_ref[...]], axis=0)))


def _outproj(x, o_rwkv, o_attn, o_rwkv_tail, o_attn_tail, w, *, tm):
    M, D = x.shape
    Wd = o_rwkv.shape[1]
    tail = pl.BlockSpec(o_rwkv_tail.shape, lambda i: (0, 0))
    return pl.pallas_call(
        _outproj_kernel,
        grid=(M // tm,),
        in_specs=[pl.BlockSpec((tm, D), lambda i: (i, 0)),
                  pl.BlockSpec((tm, Wd), lambda i: (i, 0)),
                  pl.BlockSpec((tm, Wd), lambda i: (i, 0)),
                  tail, tail,
                  pl.BlockSpec(w.shape, lambda i: (0, 0))],
        out_specs=pl.BlockSpec((tm, D), lambda i: (i, 0)),
        out_shape=jax.ShapeDtypeStruct((M, D), F32),
        compiler_params=_params("arbitrary"),
        name="outproj",
    )(x, o_rwkv, o_attn, o_rwkv_tail, o_attn_tail, w)


def _head_ones():
    r = lax.broadcasted_iota(jnp.int32, (LANES, LANES), 0) // HEAD_DIM
    c = lax.broadcasted_iota(jnp.int32, (LANES, LANES), 1) // HEAD_DIM
    return (r == c).astype(F32)


def _head_sum(x):
    ones = _head_ones().astype(BF16)
    terms = _split_terms(x, 2)
    parts = [sum(jnp.dot(t[:, j * LANES:(j + 1) * LANES], ones, preferred_element_type=F32) for t in terms)
             for j in range(x.shape[1] // LANES)]
    return parts[0] if len(parts) == 1 else jnp.concatenate(parts, axis=-1)


def _rwkv_prep_math(p_r, p_k, p_v, p_l, q_r, q_k, q_v, q_l, w):
    (mu_r, mu_k, mu_v, mu_l, dec_up, dec_base, icl_up, icl_base, gate_up, k_k, k_a) = w
    r = p_r + mu_r * (q_r - p_r)
    k = p_k + mu_k * (q_k - p_k)
    v = p_v + mu_v * (q_v - p_v)
    ul = p_l + mu_l * (q_l - p_l)
    w_log = -jax.nn.softplus(-(dec_base + _dot(jnp.tanh(ul), dec_up))) - 0.5
    lw = -jnp.exp(w_log)
    a = jax.nn.sigmoid(icl_base + _dot(ul, icl_up))
    g = _dot(jax.nn.sigmoid(ul), gate_up)
    kk = k * k_k
    kk = kk / jnp.maximum(jnp.sqrt(_head_sum(kk * kk)), 1e-12)
    k2 = k * (1.0 + (a - 1.0) * k_a)
    return r, lw, k2, v, kk, kk * a, g


def _prep_sample_kernel(pr, pk, pv, pL, qr, qk, qv, qL, *rest):
    w_refs, out_refs = rest[:11], rest[11:]
    outs = _rwkv_prep_math(pr[...], pk[...], pv[...], pL[...], qr[...], qk[...], qv[...], qL[...],
                           tuple(r[...] for r in w_refs))
    for o_ref, o in zip(out_refs, outs):
        o_ref[...] = o


def _prep_weight_specs(w, nargs):
    zero = (lambda b, i: (0, 0)) if nargs == 2 else (lambda i: (0, 0))
    return [pl.BlockSpec(a.shape, zero) for a in w]


def _prep_sample(p_all, shift_r, shift_k, shift_v, shift_l, w, *, row0, rows):
    W = RWKV_WIDTH
    rb = row0 // rows

    def cur(col_block, width):
        return pl.BlockSpec((rows, width), lambda i: (rb, col_block))

    def full(width):
        return pl.BlockSpec((rows, width), lambda i: (0, 0))

    in_specs = [cur(0, W), cur(1, W), cur(2, W), cur(COL_LORA // LORA_PAD, LORA_PAD),
                full(W), full(W), full(W), full(LORA_PAD)]
    in_specs += _prep_weight_specs(w, 1)
    out = jax.ShapeDtypeStruct((rows, W), F32)
    return pl.pallas_call(
        _prep_sample_kernel,
        grid=(1,),
        in_specs=in_specs,
        out_specs=[pl.BlockSpec((rows, W), lambda i: (0, 0))] * 7,
        out_shape=[out] * 7,
        compiler_params=_params("arbitrary"),
        name="rwkv_prep_sample",
    )(p_all, p_all, p_all, p_all, shift_r, shift_k, shift_v, shift_l, *w)


def _stack_heads(x):
    lane = lax.broadcasted_iota(jnp.int32, x.shape, 1)
    return jnp.concatenate([jnp.where(lane < HEAD_DIM, x, 0.0), jnp.where(lane >= HEAD_DIM, x, 0.0)], axis=0)


def _split_terms(x, n):
    terms = []
    for _ in range(n):
        hi = x.astype(BF16)
        terms.append(hi)
        x = x - hi.astype(F32)
    return terms


def _rwkv_chunk_pair(rg, nag, bi, ki, bend, kend, v, etot, st, consts):
    C = CHUNK
    strict, incl, eye = consts
    rg, nag, bi, ki, bend, kend, vs = (_stack_heads(a) for a in (rg, nag, bi, ki, bend, kend, v))

    gram = _dot_nt(jnp.concatenate([nag, rg], axis=0), jnp.concatenate([bi, ki], axis=0))
    yield
    a_ab = jnp.where(strict, gram[:2 * C, :2 * C], 0.0)
    a_ak = jnp.where(strict, gram[:2 * C, 2 * C:], 0.0)
    a_rb = jnp.where(incl, gram[2 * C:, :2 * C], 0.0)
    a_rk = jnp.where(incl, gram[2 * C:, 2 * C:], 0.0)

    z = _dot_nt(nag, st) + _dot(a_ak, vs)
    o2 = _dot_nt(rg, st) + _dot(a_rk, vs)
    yield

    pw = a_ab
    inv = eye + pw
    span = 1
    while span * 2 < C:
        pw = _dot(pw, pw)
        yield
        inv = inv + _dot(inv, pw)
        span *= 2
    yield

    u = _dot(inv, z)
    yield
    o2 = o2 + _dot(a_rb, u)
    st_new = st * etot + _dot(u.T, bend) + _dot(vs.T, kend)
    yield o2[:C, :] + o2[C:, :], st_new


def _rwkv_scan_kernel(pr, pk, pv, pL, qr, qk, qv, qL, *rest):
    w_refs = rest[:11]
    rk_ref, lng_ref, lnb_ref, o_ref, s_out_ref = rest[11:16]
    chunk_refs = rest[16:24]
    bonus_ref, gate_ref, st_ref = rest[24:27]
    C = CHUNK
    n_chunks = pr.shape[0] // C
    pairs = pr.shape[1] // LANES
    first = pl.program_id(1) == 0

    @pl.when(first)
    def _():
        st_ref[...] = jnp.zeros_like(st_ref)

    def prev_rows(cur_ref, tail_ref):
        cur = cur_ref[...]
        tail = jnp.where(first, 0.0, tail_ref[7:8, :])
        row0 = lax.broadcasted_iota(jnp.int32, cur.shape, 0) == 0
        return jnp.where(row0, tail, pltpu.roll(cur, 1, axis=0))

    r, lw, k, v, kk, b, g = _rwkv_prep_math(
        pr[...], pk[...], pv[...], pL[...],
        prev_rows(pr, qr), prev_rows(pk, qk), prev_rows(pv, qv), prev_rows(pL, qL),
        tuple(ref[...] for ref in w_refs))

    tb = pr.shape[0]
    ti = lax.broadcasted_iota(jnp.int32, (tb, tb), 0)
    tj = lax.broadcasted_iota(jnp.int32, (tb, tb), 1)
    same_chunk = (ti // C) == (tj // C)
    lw_terms = _split_terms(lw, 3)
    tri = (same_chunk & (tj <= ti)).astype(BF16)
    cum = sum(jnp.dot(tri, t, preferred_element_type=F32) for t in lw_terms)
    tot = sum(jnp.dot(same_chunk.astype(BF16), t, preferred_element_type=F32) for t in lw_terms)
    ginv = jnp.exp(-cum)
    gend = jnp.exp(tot - cum)
    scaled = (r * jnp.exp(cum), -kk * jnp.exp(cum - lw), b * ginv, k * ginv, b * gend, k * gend, v, jnp.exp(tot))
    for ref, val in zip(chunk_refs, scaled):
        ref[...] = val
    bonus_ref[...] = _head_sum(r * k * rk_ref[...]) * v
    gate_ref[...] = g

    row = lax.broadcasted_iota(jnp.int32, (2 * C, 2 * C), 0)
    col = lax.broadcasted_iota(jnp.int32, (2 * C, 2 * C), 1)
    consts = (col < row, col <= row, (col == row).astype(F32))

    def chunk(c, carry):
        rows = pl.ds(pl.multiple_of(c * C, C), C)
        def pair(j):
            lanes = slice(j * LANES, (j + 1) * LANES)
            tok = [ref[rows, lanes] for ref in chunk_refs]
            tok[-1] = tok[-1][:1, :]
            return _rwkv_chunk_pair(*tok, st_ref[j], consts)

        stages = [pair(j) for j in range(pairs)]
        results = [None] * pairs
        while results[-1] is None:
            for j, gen in enumerate(stages):
                results[j] = next(gen)
        for j, (o, st_new) in enumerate(results):
            o_ref[rows, j * LANES:(j + 1) * LANES] = o
            st_ref[j] = st_new
        return carry

    lax.fori_loop(0, n_chunks, chunk, 0)

    o = o_ref[...]
    mean = _head_sum(o) * (1.0 / HEAD_DIM)
    d = o - mean
    var = _head_sum(d * d) * (1.0 / HEAD_DIM)
    o_ref[...] = (d * lax.rsqrt(var + GN_EPS) * lng_ref[...] + lnb_ref[...] + bonus_ref[...]) * gate_ref[...]

    @pl.when(pl.program_id(1) == pl.num_programs(1) - 1)
    def _():
        for j in range(pairs):
            st = st_ref[j]
            s_out_ref[0, 2 * j] = st[:HEAD_DIM, :HEAD_DIM]
            s_out_ref[0, 2 * j + 1] = st[HEAD_DIM:, HEAD_DIM:]


def _rwkv_scan(p_all, prep_w, r_k, lnx_g, lnx_b, *, batch, seq, tb):
    nt = seq // tb
    W = RWKV_WIDTH
    pairs = W // LANES

    def cur(col_block, width):
        return pl.BlockSpec((tb, width), lambda b, i: (b * nt + i, col_block))

    def tail(col_block, width):
        return pl.BlockSpec((8, width), lambda b, i: (jnp.maximum((b * seq + i * tb) // 8 - 1, 0), col_block))

    lora_block = COL_LORA // LORA_PAD
    in_specs = [cur(0, W), cur(1, W), cur(2, W), cur(lora_block, LORA_PAD),
                tail(0, W), tail(1, W), tail(2, W), tail(lora_block, LORA_PAD)]
    in_specs += _prep_weight_specs(prep_w, 2)
    in_specs += [pl.BlockSpec((1, W), lambda b, i: (0, 0))] * 3
    return pl.pallas_call(
        _rwkv_scan_kernel,
        grid=(batch, nt),
        in_specs=in_specs,
        out_specs=[pl.BlockSpec((tb, W), lambda b, i: (b * nt + i, 0)),
                   pl.BlockSpec((1, N_HEADS, HEAD_DIM, HEAD_DIM), lambda b, i: (b, 0, 0, 0))],
        out_shape=[jax.ShapeDtypeStruct((batch * seq, W), F32),
                   jax.ShapeDtypeStruct((batch, N_HEADS, HEAD_DIM, HEAD_DIM), F32)],
        scratch_shapes=[pltpu.VMEM((tb, W), F32)] * 10 + [pltpu.VMEM((pairs, LANES, LANES), F32)],
        compiler_params=_params("parallel", "arbitrary"),
        name="rwkv_scan",
    )(*([p_all] * 8), *prep_w, r_k, lnx_g, lnx_b)


def _rwkv_step_kernel(s_ref, r_ref, lw_ref, k_ref, v_ref, kk_ref, b_ref, g_ref, rk_ref, lng_ref, lnb_ref,
                      o_ref, s_out_ref):
    s = s_ref[...]
    r, k, v, kk, b, g = (ref[...] for ref in (r_ref, k_ref, v_ref, kk_ref, b_ref, g_ref))
    w = jnp.exp(lw_ref[...])
    shape = (HEAD_DIM, HEAD_DIM)
    eye = lax.broadcasted_iota(jnp.int32, shape, 0) == lax.broadcasted_iota(jnp.int32, shape, 1)
    sa = jnp.sum(s * (-kk), axis=-1, keepdims=True)
    v_col = jnp.sum(jnp.where(eye, v, 0.0), axis=-1, keepdims=True)
    s_new = s * w + sa * b + v_col * k
    s_out_ref[...] = s_new
    o_col = jnp.sum(s_new * r, axis=-1, keepdims=True)
    o = jnp.sum(jnp.where(eye, o_col, 0.0), axis=-2, keepdims=True)
    mean = jnp.mean(o, axis=-1, keepdims=True)
    d = o - mean
    var = jnp.mean(d * d, axis=-1, keepdims=True)
    on = d * lax.rsqrt(var + GN_EPS) * lng_ref[...] + lnb_ref[...]
    bonus = jnp.sum(r * k * rk_ref[...], axis=-1, keepdims=True) * v
    o_ref[...] = (on + bonus) * g


def _rwkv_step(state, r, lw, k, v, kk, b, g, r_k, lnx_g, lnx_b, *, bt):
    B = state.shape[0]
    vec4 = lambda a: a.reshape(B, N_HEADS, 1, HEAD_DIM)
    par4 = lambda a: a.reshape(1, N_HEADS, 1, HEAD_DIM)
    st_spec = pl.BlockSpec((bt, N_HEADS, HEAD_DIM, HEAD_DIM), lambda i: (i, 0, 0, 0))
    vec_spec = pl.BlockSpec((bt, N_HEADS, 1, HEAD_DIM), lambda i: (i, 0, 0, 0))
    par_spec = pl.BlockSpec((1, N_HEADS, 1, HEAD_DIM), lambda i: (0, 0, 0, 0))
    o, s_new = pl.pallas_call(
        _rwkv_step_kernel,
        grid=(B // bt,),
        in_specs=[st_spec] + [vec_spec] * 7 + [par_spec] * 3,
        out_specs=[vec_spec, st_spec],
        out_shape=[jax.ShapeDtypeStruct((B, N_HEADS, 1, HEAD_DIM), F32),
                   jax.ShapeDtypeStruct(state.shape, F32)],
        compiler_params=_params("parallel"),
        name="rwkv_step",
    )(state, *(vec4(a) for a in (r, lw, k, v, kk, b, g)), par4(r_k), par4(lnx_g), par4(lnx_b))
    return o.reshape(B, RWKV_WIDTH), s_new


def _swa_prompt_kernel(q_ref, kc_ref, kp_ref, vc_ref, vp_ref, slope_ref, sink_ref, o_ref):
    blk = q_ref.shape[0]
    has_prev = pl.program_id(1) > 0
    kj = lax.broadcasted_iota(jnp.int32, (2 * blk, blk), 0)
    qi = lax.broadcasted_iota(jnp.int32, (2 * blk, blk), 1) + blk
    dist = qi - kj
    valid = (dist >= 0) & (dist < WINDOW) & (has_prev | (kj >= blk))
    distf = dist.astype(F32)

    low_lane = lax.broadcasted_iota(jnp.int32, (2 * blk, LANES), 1) < HEAD_DIM
    low_row = lax.broadcasted_iota(jnp.int32, (LANES, 2 * blk), 0) < HEAD_DIM
    kv = []
    for grp in range(kc_ref.shape[1] // LANES):
        lanes = slice(grp * LANES, (grp + 1) * LANES)
        kcat = jnp.concatenate([kp_ref[:, lanes], kc_ref[:, lanes]], axis=0)
        kroll = pltpu.roll(kcat, HEAD_DIM, axis=1)
        vt = jnp.concatenate([vp_ref[:, lanes], vc_ref[:, lanes]], axis=0).T
        vroll = pltpu.roll(vt, HEAD_DIM, axis=0)
        kv += [(jnp.where(low_lane, kcat, 0.0).astype(BF16), jnp.where(low_lane, 0.0, kroll).astype(BF16),
                jnp.where(low_row, vt, 0.0).astype(BF16), jnp.where(low_row, 0.0, vroll).astype(BF16)),
               (jnp.where(low_lane, kroll, 0.0).astype(BF16), jnp.where(low_lane, 0.0, kcat).astype(BF16),
                jnp.where(low_row, vroll, 0.0).astype(BF16), jnp.where(low_row, 0.0, vt).astype(BF16))]
    out_low_row = lax.broadcasted_iota(jnp.int32, (LANES, blk), 0) < HEAD_DIM

    def softmax_t(s, h):
        s = jnp.where(valid, s - slope_ref[h][:, :blk] * distf, NEG_INF)
        sink = sink_ref[h][:, :blk]
        m = jnp.maximum(jnp.max(s, axis=0, keepdims=True), sink)
        e = jnp.exp(s - m)
        return e, jnp.sum(e, axis=0, keepdims=True) + jnp.exp(sink - m)

    def head_pair(i):
        k_lo, k_hi, v_lo, v_hi = kv[i // 2]
        cols = slice(i * LANES, (i + 1) * LANES)
        q = (q_ref[:, cols] * (HEAD_DIM ** -0.5)).astype(BF16)
        s_lo = lax.dot_general(k_lo, q, (((1,), (1,)), ((), ())), preferred_element_type=F32)
        s_hi = lax.dot_general(k_hi, q, (((1,), (1,)), ((), ())), preferred_element_type=F32)
        yield
        e_lo, d_lo = softmax_t(s_lo, 2 * i)
        e_hi, d_hi = softmax_t(s_hi, 2 * i + 1)
        ot = (jnp.dot(v_lo, e_lo.astype(BF16), preferred_element_type=F32)
              + jnp.dot(v_hi, e_hi.astype(BF16), preferred_element_type=F32))
        yield
        o_ref[:, cols] = (ot / jnp.where(out_low_row, d_lo, d_hi)).T
        yield

    pairs = [head_pair(i) for i in range(q_ref.shape[1] // LANES)]
    for _ in range(3):
        for gen in pairs:
            next(gen)


def _swa_prompt(p_all, slopes, sinks, *, batch, seq):
    blk = WINDOW
    nb = seq // blk
    cur = lambda col: (lambda b, i: (b * nb + i, col))
    prev = lambda col: (lambda b, i: (b * nb + jnp.maximum(i - 1, 0), col))
    par = pl.BlockSpec((N_HEADS, 1, 2 * blk), lambda b, i: (0, 0, 0))
    return pl.pallas_call(
        _swa_prompt_kernel,
        grid=(batch, nb),
        in_specs=[pl.BlockSpec((blk, ATTN_WIDTH), cur(COL_Q // ATTN_WIDTH)),
                  pl.BlockSpec((blk, KV_WIDTH), cur(COL_AK // KV_WIDTH)),
                  pl.BlockSpec((blk, KV_WIDTH), prev(COL_AK // KV_WIDTH)),
                  pl.BlockSpec((blk, KV_WIDTH), cur(COL_AV // KV_WIDTH)),
                  pl.BlockSpec((blk, KV_WIDTH), prev(COL_AV // KV_WIDTH)),
                  par, par],
        out_specs=pl.BlockSpec((blk, ATTN_WIDTH), lambda b, i: (b * nb + i, 0)),
        out_shape=jax.ShapeDtypeStruct((batch * seq, ATTN_WIDTH), F32),
        compiler_params=_params("parallel", "arbitrary"),
        name="swa_prompt",
    )(p_all, p_all, p_all, p_all, p_all,
      jnp.broadcast_to(slopes.reshape(N_HEADS, 1, 1), (N_HEADS, 1, 2 * blk)),
      jnp.broadcast_to(sinks.reshape(N_HEADS, 1, 1), (N_HEADS, 1, 2 * blk)))


def _swa_sample_kernel(q_ref, kn_ref, vn_ref, ck_ref, cv_ref, slope_ref, sink_ref, o_ref, kw_ref, vw_ref):
    bt = q_ref.shape[0]
    win = ck_ref.shape[1]
    row = lax.broadcasted_iota(jnp.int32, (win, KV_WIDTH), 0)
    head_kv = lax.broadcasted_iota(jnp.int32, (N_HEADS, KV_WIDTH), 0) // GQA_GROUP
    lane_kv = lax.broadcasted_iota(jnp.int32, (N_HEADS, KV_WIDTH), 1) // HEAD_DIM
    own = head_kv == lane_kv
    dist = (win - 1 - lax.broadcasted_iota(jnp.int32, (N_HEADS, win), 1)).astype(F32)
    bias = slope_ref[...] * dist
    sink = sink_ref[...][:, :1]
    for i in range(bt):
        keys = jnp.where(row == win - 1, kn_ref[i:i + 1, :], pltpu.roll(ck_ref[i], win - 1, axis=0))
        vals = jnp.where(row == win - 1, vn_ref[i:i + 1, :], pltpu.roll(cv_ref[i], win - 1, axis=0))
        kw_ref[i] = keys
        vw_ref[i] = vals
        q = q_ref[i]
        qbd = jnp.where(own, jnp.concatenate([q] * N_KV_HEADS, axis=-1), 0.0)
        s = _dot_nt(qbd, keys) * (HEAD_DIM ** -0.5) - bias
        m = jnp.maximum(jnp.max(s, axis=-1, keepdims=True), sink)
        e = jnp.exp(s - m)
        denom = jnp.sum(e, axis=-1, keepdims=True) + jnp.exp(sink - m)
        o2 = jnp.where(own, _dot(e, vals), 0.0)
        o = o2[:, :HEAD_DIM]
        for c in range(1, N_KV_HEADS):
            o = o + o2[:, c * HEAD_DIM:(c + 1) * HEAD_DIM]
        o_ref[i] = o / denom


def _swa_sample(q, k_new, v_new, cache_k, cache_v, slopes, sinks, *, bt):
    B, win = cache_k.shape[0], cache_k.shape[1]
    q3 = pl.BlockSpec((bt, N_HEADS, HEAD_DIM), lambda i: (i, 0, 0))
    new = pl.BlockSpec((bt, KV_WIDTH), lambda i: (i, 0))
    cache = pl.BlockSpec((bt, win, KV_WIDTH), lambda i: (i, 0, 0))
    par = pl.BlockSpec((N_HEADS, win), lambda i: (0, 0))
    o, kw, vw = pl.pallas_call(
        _swa_sample_kernel,
        grid=(B // bt,),
        in_specs=[q3, new, new, cache, cache, par, par],
        out_specs=[q3, cache, cache],
        out_shape=[jax.ShapeDtypeStruct((B, N_HEADS, HEAD_DIM), F32),
                   jax.ShapeDtypeStruct((B, win, KV_WIDTH), F32),
                   jax.ShapeDtypeStruct((B, win, KV_WIDTH), F32)],
        compiler_params=_params("parallel"),
        name="swa_sample",
    )(q.reshape(B, N_HEADS, HEAD_DIM), k_new, v_new,
      cache_k.reshape(B, win, KV_WIDTH), cache_v.reshape(B, win, KV_WIDTH),
      jnp.broadcast_to(slopes.reshape(N_HEADS, 1), (N_HEADS, win)),
      jnp.broadcast_to(sinks.reshape(N_HEADS, 1), (N_HEADS, win)))
    return o.reshape(B, ATTN_WIDTH), kw, vw


def _permute_in_cols(a):
    attn_proj = ATTN_WIDTH + 2 * KV_WIDTH
    rkv = a[..., attn_proj:attn_proj + 3 * RWKV_WIDTH]
    lora = a[..., attn_proj + 3 * RWKV_WIDTH:]
    pad = jnp.zeros(a.shape[:-1] + (LORA_PAD - LORA_WIDTH,), a.dtype)
    return jnp.concatenate([rkv, a[..., :attn_proj], lora, pad], axis=-1)


def _lora_rows(w, start):
    return jnp.zeros((LORA_PAD, w.shape[1]), F32).at[start:start + w.shape[0]].set(w).astype(BF16)


def kernel(x_prompt, x_sample, state_rwkv_shift, state_rwkv_wkv, cache_swa_k, cache_swa_v, ffn1_norm, ffn1_w_gate, ffn1_w_up, ffn1_w_down, mix_norm, w_in, rwkv_mu, rwkv_decay_up, rwkv_decay_base, rwkv_iclr_up, rwkv_iclr_base, rwkv_gate_up, rwkv_k_k, rwkv_k_a, rwkv_r_k, rwkv_lnx_g, rwkv_lnx_b, attn_sinks, w_out, ffn2_norm, ffn2_w_gate, ffn2_w_up, ffn2_w_down, final_norm):
    batch, seq, d_model = x_prompt.shape
    dec_batch = x_sample.shape[0]
    depth = ffn1_norm.shape[0]
    n_prompt = batch * seq
    win = cache_swa_k.shape[2]
    assert x_sample.shape[1] == 1 and seq % 512 == 0 and win == WINDOW and n_prompt % dec_batch == 0
    tm, tm_ffn, tf, tm_in = 640, 832, 512, 1664
    assert all((n_prompt + dec_batch) % t == 0 for t in (tm, tm_ffn, tm_in))

    x = x_prompt.reshape(n_prompt, d_model)
    slopes = jnp.exp2(-8.0 * jnp.arange(1, N_HEADS + 1, dtype=F32) / N_HEADS)
    outs = [[] for _ in range(8)]
    for l in range(depth):
        x = _ffn(x, ffn1_norm[l], ffn1_w_gate[l], ffn1_w_up[l], ffn1_w_down[l], tm=tm_ffn, tf=tf,
                 x_tail=x_sample.reshape(dec_batch, d_model) if l == 0 else None)
        p_all = _inproj(x, mix_norm[l], w_in[l].T, tm=tm_in)

        mu = _permute_in_cols(jnp.concatenate([jnp.zeros((ATTN_WIDTH + 2 * KV_WIDTH,), F32), rwkv_mu[l]]))
        row = lambda a: a.reshape(1, -1)
        prep_w = (row(mu[COL_R:COL_K]), row(mu[COL_K:COL_V]), row(mu[COL_V:COL_Q]), row(mu[COL_LORA:]),
                  _lora_rows(rwkv_decay_up[l], 0), row(rwkv_decay_base[l]),
                  _lora_rows(rwkv_iclr_up[l], 64), row(rwkv_iclr_base[l]),
                  _lora_rows(rwkv_gate_up[l], 128), row(rwkv_k_k[l]), row(rwkv_k_a[l]))
        head_par = (row(rwkv_r_k[l]), row(rwkv_lnx_g[l]), row(rwkv_lnx_b[l]))

        o_rwkv_p, wkv_p = _rwkv_scan(p_all, prep_w, *head_par, batch=batch, seq=seq, tb=256)
        o_attn_p = _swa_prompt(p_all, slopes, attn_sinks[l], batch=batch, seq=seq)

        sh = _permute_in_cols(jnp.concatenate(
            [jnp.zeros((dec_batch, ATTN_WIDTH + 2 * KV_WIDTH), F32), state_rwkv_shift[l]], axis=-1))
        prep_s = _prep_sample(p_all, sh[:, COL_R:COL_K], sh[:, COL_K:COL_V], sh[:, COL_V:COL_Q], sh[:, COL_LORA:],
                              prep_w, row0=n_prompt, rows=dec_batch)
        o_rwkv_s, wkv_s = _rwkv_step(state_rwkv_wkv[l], *prep_s, *head_par, bt=8)
        p_s = p_all[n_prompt:]
        o_attn_s, kwin_s, vwin_s = _swa_sample(p_s[:, COL_Q:COL_AK], p_s[:, COL_AK:COL_AV], p_s[:, COL_AV:COL_LORA],
                                               cache_swa_k[l], cache_swa_v[l], slopes, attn_sinks[l], bt=8)

        x = _outproj(x, o_rwkv_p, o_attn_p, o_rwkv_s, o_attn_s, w_out[l].astype(BF16), tm=tm)
        last_layer = l == depth - 1
        x = _ffn(x, ffn2_norm[l], ffn2_w_gate[l], ffn2_w_up[l], ffn2_w_down[l], final_norm if last_layer else None,
                 tm=tm_ffn, tf=tf, split_tail=dec_batch if last_layer else 0)

        wb = min(WINDOW, seq)
        tails = [p_all[(b + 1) * seq - wb:(b + 1) * seq] for b in range(batch)]
        last = jnp.concatenate([t[wb - 1:] for t in tails], axis=0)
        shift_cols = lambda a: jnp.concatenate([a[:, COL_R:COL_Q], a[:, COL_LORA:COL_LORA + LORA_WIDTH]], axis=-1)
        for lst, val in zip(outs, (
                shift_cols(last), wkv_p,
                jnp.stack([t[:, COL_AK:COL_AV] for t in tails]).reshape(batch, wb, N_KV_HEADS, HEAD_DIM),
                jnp.stack([t[:, COL_AV:COL_LORA] for t in tails]).reshape(batch, wb, N_KV_HEADS, HEAD_DIM),
                shift_cols(p_s), wkv_s,
                kwin_s.reshape(dec_batch, win, N_KV_HEADS, HEAD_DIM),
                vwin_s.reshape(dec_batch, win, N_KV_HEADS, HEAD_DIM))):
            lst.append(val)

    y_prompt, y_sample = x
    return ((y_prompt.reshape(batch, seq, d_model), y_sample.reshape(dec_batch, 1, d_model))
            + tuple(jnp.stack(o) for o in outs))
```

```python
import functools

import jax
import jax.numpy as jnp
from jax import lax
from jax.experimental import pallas as pl
from jax.experimental.pallas import tpu as pltpu

F32 = jnp.float32
BF16 = jnp.bfloat16
HIGHEST = lax.Precision.HIGHEST

HEAD_DIM = 64
N_HEADS = 16
N_KV_HEADS = 4
GQA_GROUP = 4
RWKV_WIDTH = 1024
ATTN_WIDTH = 1024
KV_WIDTH = 256
WINDOW = 128
LORA_WIDTH = 64 + 64 + 160
LORA_PAD = 384
RMS_EPS = 1e-6
GN_EPS = 64e-5
NEG_INF = -1e30

COL_R, COL_K, COL_V = 0, 1024, 2048
COL_Q = 3072
COL_AK = 4096
COL_AV = 4352
COL_LORA = 4608
IN_BLOCK = 512
IN_PAD = COL_LORA + IN_BLOCK

LANES = 128
CHUNK = 64
PREP_GROUP = 1024
VMEM_LIMIT = 60 * 1024 * 1024


def _dot(a, b):
    return jnp.dot(a.astype(BF16), b.astype(BF16), preferred_element_type=F32)


def _dot_nt(a, b):
    return lax.dot_general(a.astype(BF16), b.astype(BF16), (((1,), (1,)), ((), ())),
                           preferred_element_type=F32)


def _dot_hi(a, b):
    return jnp.dot(a, b, preferred_element_type=F32, precision=HIGHEST)


def _rms(x, g):
    return x * lax.rsqrt(jnp.mean(x * x, axis=-1, keepdims=True) + RMS_EPS) * g


def _params(*sem):
    return pltpu.CompilerParams(dimension_semantics=sem, vmem_limit_bytes=VMEM_LIMIT)


def _ffn_kernel(*refs, final, d_ff, tail_in, tail_out):
    refs = list(refs)
    x_ref = refs.pop(0)
    xt_ref = refs.pop(0) if tail_in else None
    g_ref, wg_ref, wu_ref, wd_ref = (refs.pop(0) for _ in range(4))
    fg_ref = refs.pop(0) if final else None
    o_ref = refs.pop(0)
    ot_ref = refs.pop(0) if tail_out else None
    (xn_ref,) = refs
    f = pl.program_id(1)
    last_tile = pl.program_id(0) == pl.num_programs(0) - 1
    tm = x_ref.shape[0]
    tf = wg_ref.shape[1]
    nf = -(-d_ff // tf)
    last_width = d_ff - (nf - 1) * tf

    def start(x):
        xn_ref[...] = _rms(x, g_ref[...]).astype(BF16)
        o_ref[...] = x

    if tail_in:
        own = tm - xt_ref.shape[0]
        pl.when((f == 0) & jnp.logical_not(last_tile))(lambda: start(x_ref[...]))
        pl.when((f == 0) & last_tile)(lambda: start(jnp.concatenate([x_ref[:own, :], xt_ref[...]], axis=0)))
    else:
        pl.when(f == 0)(lambda: start(x_ref[...]))

    def accumulate(width):
        xn = xn_ref[...]
        hg = jnp.dot(xn, wg_ref[:, :width].astype(BF16), preferred_element_type=F32)
        hu = jnp.dot(xn, wu_ref[:, :width].astype(BF16), preferred_element_type=F32)
        h = (hg * jax.nn.sigmoid(hg) * (0.5 * hu)).astype(BF16)
        o_ref[...] += jnp.dot(h, wd_ref[:width, :].astype(BF16), preferred_element_type=F32)

    if last_width == tf:
        accumulate(tf)
    else:
        pl.when(f < nf - 1)(lambda: accumulate(tf))
        pl.when(f == nf - 1)(lambda: accumulate(last_width))

    if final:
        @pl.when(f == nf - 1)
        def _():
            o_ref[...] = _rms(o_ref[...], fg_ref[...])

    if tail_out:
        @pl.when((f == nf - 1) & last_tile)
        def _():
            ot_ref[...] = o_ref[tm - ot_ref.shape[0]:, :]


def _ffn(x, norm_g, wg, wu, wd, final_g=None, *, tm, tf, x_tail=None, split_tail=0):
    D = x.shape[1]
    M = x.shape[0] + (0 if x_tail is None else x_tail.shape[0])
    d_ff = wg.shape[1]
    final = final_g is not None
    const = lambda i, f: (0, 0)
    in_specs = [pl.BlockSpec((tm, D), lambda i, f: (i, 0))]
    args = [x]
    if x_tail is not None:
        in_specs.append(pl.BlockSpec(x_tail.shape, const))
        args.append(x_tail)
    in_specs += [pl.BlockSpec((1, D), const),
                 pl.BlockSpec((D, tf), lambda i, f: (0, f)),
                 pl.BlockSpec((D, tf), lambda i, f: (0, f)),
                 pl.BlockSpec((tf, D), lambda i, f: (f, 0))]
    args += [norm_g.reshape(1, D), wg, wu, wd]
    if final:
        in_specs.append(pl.BlockSpec((1, D), const))
        args.append(final_g.reshape(1, D))
    out_specs = [pl.BlockSpec((tm, D), lambda i, f: (i, 0))]
    out_shape = [jax.ShapeDtypeStruct((M - split_tail, D), F32)]
    if split_tail:
        out_specs.append(pl.BlockSpec((split_tail, D), const))
        out_shape.append(jax.ShapeDtypeStruct((split_tail, D), F32))
    out = pl.pallas_call(
        functools.partial(_ffn_kernel, final=final, d_ff=d_ff, tail_in=x_tail is not None,
                          tail_out=bool(split_tail)),
        grid=(M // tm, pl.cdiv(d_ff, tf)),
        in_specs=in_specs,
        out_specs=out_specs,
        out_shape=out_shape,
        scratch_shapes=[pltpu.VMEM((tm, D), BF16)],
        compiler_params=_params("arbitrary", "arbitrary"),
        name="ffn",
    )(*args)
    return out if split_tail else out[0]


def _inproj_kernel(x_ref, g_ref, w_ref, o_ref, xn_ref, *, last_valid):
    j = pl.program_id(1)
    nj = pl.num_programs(1)

    @pl.when(j == 0)
    def _():
        xn_ref[...] = _rms(x_ref[...], g_ref[...]).astype(BF16)

    def run(wt):
        o_ref[...] = lax.dot_general(xn_ref[...], wt.astype(BF16), (((1,), (1,)), ((), ())),
                                     preferred_element_type=F32)

    row = lax.broadcasted_iota(jnp.int32, w_ref.shape, 0)
    pl.when(j < nj - 1)(lambda: run(w_ref[...]))
    pl.when(j == nj - 1)(lambda: run(jnp.where(row < last_valid, w_ref[...], 0.0)))


def _inproj(x, norm_g, w_t, *, tm):
    M, D = x.shape
    n_src = w_t.shape[0]
    tn = IN_BLOCK
    nj = IN_PAD // tn
    assert pl.cdiv(n_src, tn) == nj and COL_LORA == (nj - 1) * tn

    def src_block(j):
        return jnp.where(j < 6, j + 3, jnp.where(j < 9, j - 6, j))

    return pl.pallas_call(
        functools.partial(_inproj_kernel, last_valid=n_src - COL_LORA),
        grid=(M // tm, nj),
        in_specs=[pl.BlockSpec((tm, D), lambda i, j: (i, 0)),
                  pl.BlockSpec((1, D), lambda i, j: (0, 0)),
                  pl.BlockSpec((tn, D), lambda i, j: (src_block(j), 0))],
        out_specs=pl.BlockSpec((tm, tn), lambda i, j: (i, j)),
        out_shape=jax.ShapeDtypeStruct((M, IN_PAD), F32),
        scratch_shapes=[pltpu.VMEM((tm, D), BF16)],
        compiler_params=_params("parallel", "arbitrary"),
        name="inproj",
    )(x, norm_g.reshape(1, D), w_t)


def _outproj_kernel(x_ref, oa_ref, ob_ref, oat_ref, obt_ref, w_ref, o_ref):
    half = oa_ref.shape[1]
    own = x_ref.shape[0] - oat_ref.shape[0]
    last_tile = pl.program_id(0) == pl.num_programs(0) - 1

    def run(oa, ob):
        o_ref[...] = (x_ref[...]
                      + jnp.dot(oa.astype(BF16), w_ref[:half, :], preferred_element_type=F32)
                      + jnp.dot(ob.astype(BF16), w_ref[half:, :], preferred_element_type=F32))

    pl.when(jnp.logical_not(last_tile))(lambda: run(oa_ref[...], ob_ref[...]))
    pl.when(last_tile)(lambda: run(jnp.concatenate([oa_ref[:own, :], oat_ref[...]], axis=0),
                                   jnp.concatenate([ob_ref[:own, :], obt_ref[...]], axis=0)))


def _outproj(x, o_rwkv, o_attn, o_rwkv_tail, o_attn_tail, w, *, tm):
    M, D = x.shape
    Wd = o_rwkv.shape[1]
    tail = pl.BlockSpec(o_rwkv_tail.shape, lambda i: (0, 0))
    return pl.pallas_call(
        _outproj_kernel,
        grid=(M // tm,),
        in_specs=[pl.BlockSpec((tm, D), lambda i: (i, 0)),
                  pl.BlockSpec((tm, Wd), lambda i: (i, 0)),
                  pl.BlockSpec((tm, Wd), lambda i: (i, 0)),
                  tail, tail,
                  pl.BlockSpec(w.shape, lambda i: (0, 0))],
        out_specs=pl.BlockSpec((tm, D), lambda i: (i, 0)),
        out_shape=jax.ShapeDtypeStruct((M, D), F32),
        compiler_params=_params("arbitrary"),
        name="outproj",
    )(x, o_rwkv, o_attn, o_rwkv_tail, o_attn_tail, w)


def _head_ones():
    r = lax.broadcasted_iota(jnp.int32, (LANES, LANES), 0) // HEAD_DIM
    c = lax.broadcasted_iota(jnp.int32, (LANES, LANES), 1) // HEAD_DIM
    return (r == c).astype(F32)


def _head_sum(x):
    ones = _head_ones().astype(BF16)
    terms = _split_terms(x, 2)
    parts = [sum(jnp.dot(t[:, j * LANES:(j + 1) * LANES], ones, preferred_element_type=F32) for t in terms)
             for j in range(x.shape[1] // LANES)]
    return parts[0] if len(parts) == 1 else jnp.concatenate(parts, axis=-1)


def _rwkv_prep_math(p_r, p_k, p_v, p_l, q_r, q_k, q_v, q_l, w):
    (mu_r, mu_k, mu_v, mu_l, dec_up, dec_base, icl_up, icl_base, gate_up, k_k, k_a) = w
    r = p_r + mu_r * (q_r - p_r)
    k = p_k + mu_k * (q_k - p_k)
    v = p_v + mu_v * (q_v - p_v)
    ul = p_l + mu_l * (q_l - p_l)
    w_log = -jax.nn.softplus(-(dec_base + _dot(jnp.tanh(ul), dec_up))) - 0.5
    lw = -jnp.exp(w_log)
    a = jax.nn.sigmoid(icl_base + _dot(ul, icl_up))
    g = _dot(jax.nn.sigmoid(ul), gate_up)
    kk = k * k_k
    kk = kk * jnp.minimum(lax.rsqrt(_head_sum(kk * kk)), 1e12)
    k2 = k * (1.0 + (a - 1.0) * k_a)
    return r, lw, k2, v, kk, kk * a, g


def _prep_sample_kernel(pr, pk, pv, pL, qr, qk, qv, qL, *rest):
    w_refs, out_refs = rest[:11], rest[11:]
    outs = _rwkv_prep_math(pr[...], pk[...], pv[...], pL[...], qr[...], qk[...], qv[...], qL[...],
                           tuple(r[...] for r in w_refs))
    for o_ref, o in zip(out_refs, outs):
        o_ref[...] = o


def _prep_weight_specs(w, nargs):
    zero = (lambda b, i: (0, 0)) if nargs == 2 else (lambda i: (0, 0))
    return [pl.BlockSpec(a.shape, zero) for a in w]


def _prep_sample(p_all, shift_r, shift_k, shift_v, shift_l, w, *, row0, rows):
    W = RWKV_WIDTH
    rb = row0 // rows

    def cur(col_block, width):
        return pl.BlockSpec((rows, width), lambda i: (rb, col_block))

    def full(width):
        return pl.BlockSpec((rows, width), lambda i: (0, 0))

    in_specs = [cur(0, W), cur(1, W), cur(2, W), cur(COL_LORA // LORA_PAD, LORA_PAD),
                full(W), full(W), full(W), full(LORA_PAD)]
    in_specs += _prep_weight_specs(w, 1)
    out = jax.ShapeDtypeStruct((rows, W), F32)
    return pl.pallas_call(
        _prep_sample_kernel,
        grid=(1,),
        in_specs=in_specs,
        out_specs=[pl.BlockSpec((rows, W), lambda i: (0, 0))] * 7,
        out_shape=[out] * 7,
        compiler_params=_params("arbitrary"),
        name="rwkv_prep_sample",
    )(p_all, p_all, p_all, p_all, shift_r, shift_k, shift_v, shift_l, *w)


def _stack_heads(x):
    lane = lax.broadcasted_iota(jnp.int32, x.shape, 1)
    return jnp.concatenate([jnp.where(lane < HEAD_DIM, x, 0.0), jnp.where(lane >= HEAD_DIM, x, 0.0)], axis=0)


def _split_terms(x, n):
    terms = []
    for _ in range(n):
        hi = x.astype(BF16)
        terms.append(hi)
        x = x - hi.astype(F32)
    return terms


def _rwkv_chunk_pair(rg, nag, bi, ki, bend, kend, v, etot, st, consts):
    C = CHUNK
    strict, incl, eye = consts
    rg, nag, bi, ki, bend, kend, vs = (_stack_heads(a) for a in (rg, nag, bi, ki, bend, kend, v))

    gram = _dot_nt(jnp.concatenate([nag, rg], axis=0), jnp.concatenate([bi, ki], axis=0))
    yield
    a_ab = jnp.where(strict, gram[:2 * C, :2 * C], 0.0)
    a_ak = jnp.where(strict, gram[:2 * C, 2 * C:], 0.0)
    a_rb = jnp.where(incl, gram[2 * C:, :2 * C], 0.0)
    a_rk = jnp.where(incl, gram[2 * C:, 2 * C:], 0.0)

    z = _dot_nt(nag, st) + _dot(a_ak, vs)
    o2 = _dot_nt(rg, st) + _dot(a_rk, vs)
    yield

    pw = a_ab
    inv = eye + pw
    span = 1
    while span * 2 < C:
        pw = _dot(pw, pw)
        yield
        inv = inv + _dot(inv, pw)
        span *= 2
    yield

    u = _dot(inv, z)
    yield
    o2 = o2 + _dot(a_rb, u)
    st_new = st * etot + _dot(u.T, bend) + _dot(vs.T, kend)
    yield o2[:C, :] + o2[C:, :], st_new


def _rwkv_scan_kernel(pr, pk, pv, pL, qr, qk, qv, qL, *rest):
    w_refs = rest[:11]
    rk_ref, lng_ref, lnb_ref, o_ref, s_out_ref = rest[11:16]
    chunk_refs = rest[16:24]
    bonus_ref, gate_ref, st_ref = rest[24:27]
    C = CHUNK
    n_chunks = pr.shape[0] // C
    pairs = pr.shape[1] // LANES
    first = pl.program_id(1) == 0

    @pl.when(first)
    def _():
        st_ref[...] = jnp.zeros_like(st_ref)

    def prev_rows(cur_ref, tail_ref, lanes=slice(None)):
        cur = cur_ref[:, lanes]
        tail = jnp.where(first, 0.0, tail_ref[7:8, lanes])
        row0 = lax.broadcasted_iota(jnp.int32, cur.shape, 0) == 0
        return jnp.where(row0, tail, pltpu.roll(cur, 1, axis=0))

    tb = pr.shape[0]
    ti = lax.broadcasted_iota(jnp.int32, (tb, tb), 0)
    tj = lax.broadcasted_iota(jnp.int32, (tb, tb), 1)
    same_chunk = (ti // C) == (tj // C)
    tri = (same_chunk & (tj <= ti)).astype(BF16)
    ones_chunk = same_chunk.astype(BF16)
    p_lora, q_lora = pL[...], prev_rows(pL, qL)

    for grp in range(pr.shape[1] // PREP_GROUP):
        gl = slice(grp * PREP_GROUP, (grp + 1) * PREP_GROUP)
        w_grp = tuple(ref[...] if idx == 3 else ref[:, gl] for idx, ref in enumerate(w_refs))
        r, lw, k, v, kk, b, g = _rwkv_prep_math(
            pr[:, gl], pk[:, gl], pv[:, gl], p_lora,
            prev_rows(pr, qr, gl), prev_rows(pk, qk, gl), prev_rows(pv, qv, gl), q_lora, w_grp)
        lw_terms = _split_terms(lw, 3)
        cum = sum(jnp.dot(tri, t, preferred_element_type=F32) for t in lw_terms)
        tot = sum(jnp.dot(ones_chunk, t, preferred_element_type=F32) for t in lw_terms)
        ginv = jnp.exp(-cum)
        gend = jnp.exp(tot - cum)
        scaled = (r * jnp.exp(cum), -kk * jnp.exp(cum - lw), b * ginv, k * ginv, b * gend, k * gend, v,
                  jnp.exp(tot))
        for ref, val in zip(chunk_refs, scaled):
            ref[:, gl] = val
        bonus_ref[:, gl] = _head_sum(r * k * rk_ref[:, gl]) * v
        gate_ref[:, gl] = g

    row = lax.broadcasted_iota(jnp.int32, (2 * C, 2 * C), 0)
    col = lax.broadcasted_iota(jnp.int32, (2 * C, 2 * C), 1)
    consts = (col < row, col <= row, (col == row).astype(F32))

    def chunk(c, carry):
        rows = pl.ds(pl.multiple_of(c * C, C), C)
        def pair(j):
            lanes = slice(j * LANES, (j + 1) * LANES)
            tok = [ref[rows, lanes] for ref in chunk_refs]
            tok[-1] = tok[-1][:1, :]
            return _rwkv_chunk_pair(*tok, st_ref[j], consts)

        stages = [pair(j) for j in range(pairs)]
        results = [None] * pairs
        while results[-1] is None:
            for j, gen in enumerate(stages):
                results[j] = next(gen)
        for j, (o, st_new) in enumerate(results):
            o_ref[rows, j * LANES:(j + 1) * LANES] = o
            st_ref[j] = st_new
        return carry

    lax.fori_loop(0, n_chunks, chunk, 0)

    for grp in range(pr.shape[1] // PREP_GROUP):
        gl = slice(grp * PREP_GROUP, (grp + 1) * PREP_GROUP)
        o = o_ref[:, gl]
        mean = _head_sum(o) * (1.0 / HEAD_DIM)
        d = o - mean
        var = _head_sum(d * d) * (1.0 / HEAD_DIM)
        o_ref[:, gl] = ((d * lax.rsqrt(var + GN_EPS) * lng_ref[:, gl] + lnb_ref[:, gl] + bonus_ref[:, gl])
                        * gate_ref[:, gl])

    @pl.when(pl.program_id(1) == pl.num_programs(1) - 1)
    def _():
        for j in range(pairs):
            st = st_ref[j]
            s_out_ref[0, 2 * j] = st[:HEAD_DIM, :HEAD_DIM]
            s_out_ref[0, 2 * j + 1] = st[HEAD_DIM:, HEAD_DIM:]


def _rwkv_scan(p_all, prep_w, r_k, lnx_g, lnx_b, *, batch, seq, tb):
    nt = seq // tb
    W = RWKV_WIDTH
    pairs = W // LANES

    def cur(col_block, width):
        return pl.BlockSpec((tb, width), lambda b, i: (b * nt + i, col_block))

    def tail(col_block, width):
        return pl.BlockSpec((8, width), lambda b, i: (jnp.maximum((b * seq + i * tb) // 8 - 1, 0), col_block))

    lora_block = COL_LORA // LORA_PAD
    in_specs = [cur(0, W), cur(1, W), cur(2, W), cur(lora_block, LORA_PAD),
                tail(0, W), tail(1, W), tail(2, W), tail(lora_block, LORA_PAD)]
    in_specs += _prep_weight_specs(prep_w, 2)
    in_specs += [pl.BlockSpec((1, W), lambda b, i: (0, 0))] * 3
    return pl.pallas_call(
        _rwkv_scan_kernel,
        grid=(batch, nt),
        in_specs=in_specs,
        out_specs=[pl.BlockSpec((tb, W), lambda b, i: (b * nt + i, 0)),
                   pl.BlockSpec((1, N_HEADS, HEAD_DIM, HEAD_DIM), lambda b, i: (b, 0, 0, 0))],
        out_shape=[jax.ShapeDtypeStruct((batch * seq, W), F32),
                   jax.ShapeDtypeStruct((batch, N_HEADS, HEAD_DIM, HEAD_DIM), F32)],
        scratch_shapes=[pltpu.VMEM((tb, W), F32)] * 10 + [pltpu.VMEM((pairs, LANES, LANES), F32)],
        compiler_params=_params("parallel", "arbitrary"),
        name="rwkv_scan",
    )(*([p_all] * 8), *prep_w, r_k, lnx_g, lnx_b)


def _rwkv_step_kernel(s_ref, r_ref, lw_ref, k_ref, v_ref, kk_ref, b_ref, g_ref, rk_ref, lng_ref, lnb_ref,
                      o_ref, s_out_ref):
    s = s_ref[...]
    r, k, v, kk, b, g = (ref[...] for ref in (r_ref, k_ref, v_ref, kk_ref, b_ref, g_ref))
    w = jnp.exp(lw_ref[...])
    shape = (HEAD_DIM, HEAD_DIM)
    eye = lax.broadcasted_iota(jnp.int32, shape, 0) == lax.broadcasted_iota(jnp.int32, shape, 1)
    sa = jnp.sum(s * (-kk), axis=-1, keepdims=True)
    v_col = jnp.sum(jnp.where(eye, v, 0.0), axis=-1, keepdims=True)
    s_new = s * w + sa * b + v_col * k
    s_out_ref[...] = s_new
    o_col = jnp.sum(s_new * r, axis=-1, keepdims=True)
    o = jnp.sum(jnp.where(eye, o_col, 0.0), axis=-2, keepdims=True)
    mean = jnp.mean(o, axis=-1, keepdims=True)
    d = o - mean
    var = jnp.mean(d * d, axis=-1, keepdims=True)
    on = d * lax.rsqrt(var + GN_EPS) * lng_ref[...] + lnb_ref[...]
    bonus = jnp.sum(r * k * rk_ref[...], axis=-1, keepdims=True) * v
    o_ref[...] = (on + bonus) * g


def _rwkv_step(state, r, lw, k, v, kk, b, g, r_k, lnx_g, lnx_b, *, bt):
    B = state.shape[0]
    vec4 = lambda a: a.reshape(B, N_HEADS, 1, HEAD_DIM)
    par4 = lambda a: a.reshape(1, N_HEADS, 1, HEAD_DIM)
    st_spec = pl.BlockSpec((bt, N_HEADS, HEAD_DIM, HEAD_DIM), lambda i: (i, 0, 0, 0))
    vec_spec = pl.BlockSpec((bt, N_HEADS, 1, HEAD_DIM), lambda i: (i, 0, 0, 0))
    par_spec = pl.BlockSpec((1, N_HEADS, 1, HEAD_DIM), lambda i: (0, 0, 0, 0))
    o, s_new = pl.pallas_call(
        _rwkv_step_kernel,
        grid=(B // bt,),
        in_specs=[st_spec] + [vec_spec] * 7 + [par_spec] * 3,
        out_specs=[vec_spec, st_spec],
        out_shape=[jax.ShapeDtypeStruct((B, N_HEADS, 1, HEAD_DIM), F32),
                   jax.ShapeDtypeStruct(state.shape, F32)],
        compiler_params=_params("parallel"),
        name="rwkv_step",
    )(state, *(vec4(a) for a in (r, lw, k, v, kk, b, g)), par4(r_k), par4(lnx_g), par4(lnx_b))
    return o.reshape(B, RWKV_WIDTH), s_new


def _swa_prompt_kernel(q_ref, kc_ref, kp_ref, vc_ref, vp_ref, slope_ref, sink_ref, o_ref):
    blk = q_ref.shape[0]
    has_prev = pl.program_id(1) > 0
    kj = lax.broadcasted_iota(jnp.int32, (2 * blk, blk), 0)
    qi = lax.broadcasted_iota(jnp.int32, (2 * blk, blk), 1) + blk
    dist = qi - kj
    valid = (dist >= 0) & (dist < WINDOW) & (has_prev | (kj >= blk))
    distf = dist.astype(F32)

    low_lane = lax.broadcasted_iota(jnp.int32, (2 * blk, LANES), 1) < HEAD_DIM
    low_row = lax.broadcasted_iota(jnp.int32, (LANES, 2 * blk), 0) < HEAD_DIM
    kv = []
    for grp in range(kc_ref.shape[1] // LANES):
        lanes = slice(grp * LANES, (grp + 1) * LANES)
        kcat = jnp.concatenate([kp_ref[:, lanes], kc_ref[:, lanes]], axis=0)
        kroll = pltpu.roll(kcat, HEAD_DIM, axis=1)
        vt = jnp.concatenate([vp_ref[:, lanes], vc_ref[:, lanes]], axis=0).T
        vroll = pltpu.roll(vt, HEAD_DIM, axis=0)
        kv += [(jnp.where(low_lane, kcat, 0.0).astype(BF16), jnp.where(low_lane, 0.0, kroll).astype(BF16),
                jnp.where(low_row, vt, 0.0).astype(BF16), jnp.where(low_row, 0.0, vroll).astype(BF16)),
               (jnp.where(low_lane, kroll, 0.0).astype(BF16), jnp.where(low_lane, 0.0, kcat).astype(BF16),
                jnp.where(low_row, vroll, 0.0).astype(BF16), jnp.where(low_row, 0.0, vt).astype(BF16))]
    out_low_row = lax.broadcasted_iota(jnp.int32, (LANES, blk), 0) < HEAD_DIM

    def softmax_t(s, h):
        s = jnp.where(valid, s - slope_ref[h][:, :blk] * distf, NEG_INF)
        sink = sink_ref[h][:, :blk]
        m = jnp.maximum(jnp.max(s, axis=0, keepdims=True), sink)
        e = jnp.exp(s - m)
        return e, jnp.sum(e, axis=0, keepdims=True) + jnp.exp(sink - m)

    def head_pair(i):
        k_lo, k_hi, v_lo, v_hi = kv[i // 2]
        cols = slice(i * LANES, (i + 1) * LANES)
        q = (q_ref[:, cols] * (HEAD_DIM ** -0.5)).astype(BF16)
        s_lo = lax.dot_general(k_lo, q, (((1,), (1,)), ((), ())), preferred_element_type=F32)
        s_hi = lax.dot_general(k_hi, q, (((1,), (1,)), ((), ())), preferred_element_type=F32)
        yield
        e_lo, d_lo = softmax_t(s_lo, 2 * i)
        e_hi, d_hi = softmax_t(s_hi, 2 * i + 1)
        ot = (jnp.dot(v_lo, e_lo.astype(BF16), preferred_element_type=F32)
              + jnp.dot(v_hi, e_hi.astype(BF16), preferred_element_type=F32))
        yield
        o_ref[:, cols] = (ot / jnp.where(out_low_row, d_lo, d_hi)).T
        yield

    pairs = [head_pair(i) for i in range(q_ref.shape[1] // LANES)]
    for _ in range(3):
        for gen in pairs:
            next(gen)


def _swa_prompt(p_all, slopes, sinks, *, batch, seq):
    blk = WINDOW
    nb = seq // blk
    cur = lambda col: (lambda b, i: (b * nb + i, col))
    prev = lambda col: (lambda b, i: (b * nb + jnp.maximum(i - 1, 0), col))
    par = pl.BlockSpec((N_HEADS, 1, 2 * blk), lambda b, i: (0, 0, 0))
    return pl.pallas_call(
        _swa_prompt_kernel,
        grid=(batch, nb),
        in_specs=[pl.BlockSpec((blk, ATTN_WIDTH), cur(COL_Q // ATTN_WIDTH)),
                  pl.BlockSpec((blk, KV_WIDTH), cur(COL_AK // KV_WIDTH)),
                  pl.BlockSpec((blk, KV_WIDTH), prev(COL_AK // KV_WIDTH)),
                  pl.BlockSpec((blk, KV_WIDTH), cur(COL_AV // KV_WIDTH)),
                  pl.BlockSpec((blk, KV_WIDTH), prev(COL_AV // KV_WIDTH)),
                  par, par],
        out_specs=pl.BlockSpec((blk, ATTN_WIDTH), lambda b, i: (b * nb + i, 0)),
        out_shape=jax.ShapeDtypeStruct((batch * seq, ATTN_WIDTH), F32),
        compiler_params=_params("parallel", "arbitrary"),
        name="swa_prompt",
    )(p_all, p_all, p_all, p_all, p_all,
      jnp.broadcast_to(slopes.reshape(N_HEADS, 1, 1), (N_HEADS, 1, 2 * blk)),
      jnp.broadcast_to(sinks.reshape(N_HEADS, 1, 1), (N_HEADS, 1, 2 * blk)))


def _swa_sample_kernel(q_ref, kn_ref, vn_ref, ck_ref, cv_ref, slope_ref, sink_ref, o_ref, kw_ref, vw_ref):
    bt = q_ref.shape[0]
    win = ck_ref.shape[1]
    row = lax.broadcasted_iota(jnp.int32, (win, KV_WIDTH), 0)
    head_kv = lax.broadcasted_iota(jnp.int32, (N_HEADS, KV_WIDTH), 0) // GQA_GROUP
    lane_kv = lax.broadcasted_iota(jnp.int32, (N_HEADS, KV_WIDTH), 1) // HEAD_DIM
    own = head_kv == lane_kv
    dist = (win - 1 - lax.broadcasted_iota(jnp.int32, (N_HEADS, win), 1)).astype(F32)
    bias = slope_ref[...] * dist
    sink = sink_ref[...][:, :1]
    for i in range(bt):
        keys = jnp.where(row == win - 1, kn_ref[i:i + 1, :], pltpu.roll(ck_ref[i], win - 1, axis=0))
        vals = jnp.where(row == win - 1, vn_ref[i:i + 1, :], pltpu.roll(cv_ref[i], win - 1, axis=0))
        kw_ref[i] = keys
        vw_ref[i] = vals
        q = q_ref[i]
        qbd = jnp.where(own, jnp.concatenate([q] * N_KV_HEADS, axis=-1), 0.0)
        s = _dot_nt(qbd, keys) * (HEAD_DIM ** -0.5) - bias
        m = jnp.maximum(jnp.max(s, axis=-1, keepdims=True), sink)
        e = jnp.exp(s - m)
        denom = jnp.sum(e, axis=-1, keepdims=True) + jnp.exp(sink - m)
        o2 = jnp.where(own, _dot(e, vals), 0.0)
        o = o2[:, :HEAD_DIM]
        for c in range(1, N_KV_HEADS):
            o = o + o2[:, c * HEAD_DIM:(c + 1) * HEAD_DIM]
        o_ref[i] = o / denom


def _swa_sample(q, k_new, v_new, cache_k, cache_v, slopes, sinks, *, bt):
    B, win = cache_k.shape[0], cache_k.shape[1]
    q3 = pl.BlockSpec((bt, N_HEADS, HEAD_DIM), lambda i: (i, 0, 0))
    new = pl.BlockSpec((bt, KV_WIDTH), lambda i: (i, 0))
    cache = pl.BlockSpec((bt, win, KV_WIDTH), lambda i: (i, 0, 0))
    par = pl.BlockSpec((N_HEADS, win), lambda i: (0, 0))
    o, kw, vw = pl.pallas_call(
        _swa_sample_kernel,
        grid=(B // bt,),
        in_specs=[q3, new, new, cache, cache, par, par],
        out_specs=[q3, cache, cache],
        out_shape=[jax.ShapeDtypeStruct((B, N_HEADS, HEAD_DIM), F32),
                   jax.ShapeDtypeStruct((B, win, KV_WIDTH), F32),
                   jax.ShapeDtypeStruct((B, win, KV_WIDTH), F32)],
        compiler_params=_params("parallel"),
        name="swa_sample",
    )(q.reshape(B, N_HEADS, HEAD_DIM), k_new, v_new,
      cache_k.reshape(B, win, KV_WIDTH), cache_v.reshape(B, win, KV_WIDTH),
      jnp.broadcast_to(slopes.reshape(N_HEADS, 1), (N_HEADS, win)),
      jnp.broadcast_to(sinks.reshape(N_HEADS, 1), (N_HEADS, win)))
    return o.reshape(B, ATTN_WIDTH), kw, vw


def _permute_in_cols(a):
    attn_proj = ATTN_WIDTH + 2 * KV_WIDTH
    rkv = a[..., attn_proj:attn_proj + 3 * RWKV_WIDTH]
    lora = a[..., attn_proj + 3 * RWKV_WIDTH:]
    pad = jnp.zeros(a.shape[:-1] + (LORA_PAD - LORA_WIDTH,), a.dtype)
    return jnp.concatenate([rkv, a[..., :attn_proj], lora, pad], axis=-1)


def _lora_rows(w, start):
    return jnp.zeros((LORA_PAD, w.shape[1]), F32).at[start:start + w.shape[0]].set(w).astype(BF16)


def kernel(x_prompt, x_sample, state_rwkv_shift, state_rwkv_wkv, cache_swa_k, cache_swa_v, ffn1_norm, ffn1_w_gate, ffn1_w_up, ffn1_w_down, mix_norm, w_in, rwkv_mu, rwkv_decay_up, rwkv_decay_base, rwkv_iclr_up, rwkv_iclr_base, rwkv_gate_up, rwkv_k_k, rwkv_k_a, rwkv_r_k, rwkv_lnx_g, rwkv_lnx_b, attn_sinks, w_out, ffn2_norm, ffn2_w_gate, ffn2_w_up, ffn2_w_down, final_norm):
    batch, seq, d_model = x_prompt.shape
    dec_batch = x_sample.shape[0]
    depth = ffn1_norm.shape[0]
    n_prompt = batch * seq
    win = cache_swa_k.shape[2]
    assert x_sample.shape[1] == 1 and seq % 512 == 0 and win == WINDOW and n_prompt % dec_batch == 0
    tm, tm_ffn, tf, tm_in = 640, 832, 512, 1664
    assert all((n_prompt + dec_batch) % t == 0 for t in (tm, tm_ffn, tm_in))

    x = x_prompt.reshape(n_prompt, d_model)
    slopes = jnp.exp2(-8.0 * jnp.arange(1, N_HEADS + 1, dtype=F32) / N_HEADS)
    outs = [[] for _ in range(8)]
    for l in range(depth):
        x = _ffn(x, ffn1_norm[l], ffn1_w_gate[l], ffn1_w_up[l], ffn1_w_down[l], tm=tm_ffn, tf=tf,
                 x_tail=x_sample.reshape(dec_batch, d_model) if l == 0 else None)
        p_all = _inproj(x, mix_norm[l], w_in[l].T, tm=tm_in)

        mu = _permute_in_cols(jnp.concatenate([jnp.zeros((ATTN_WIDTH + 2 * KV_WIDTH,), F32), rwkv_mu[l]]))
        row = lambda a: a.reshape(1, -1)
        prep_w = (row(mu[COL_R:COL_K]), row(mu[COL_K:COL_V]), row(mu[COL_V:COL_Q]), row(mu[COL_LORA:]),
                  _lora_rows(rwkv_decay_up[l], 0), row(rwkv_decay_base[l]),
                  _lora_rows(rwkv_iclr_up[l], 64), row(rwkv_iclr_base[l]),
                  _lora_rows(rwkv_gate_up[l], 128), row(rwkv_k_k[l]), row(rwkv_k_a[l]))
        head_par = (row(rwkv_r_k[l]), row(rwkv_lnx_g[l]), row(rwkv_lnx_b[l]))

        o_rwkv_p, wkv_p = _rwkv_scan(p_all, prep_w, *head_par, batch=batch, seq=seq, tb=256)
        o_attn_p = _swa_prompt(p_all, slopes, attn_sinks[l], batch=batch, seq=seq)

        sh = _permute_in_cols(jnp.concatenate(
            [jnp.zeros((dec_batch, ATTN_WIDTH + 2 * KV_WIDTH), F32), state_rwkv_shift[l]], axis=-1))
        prep_s = _prep_sample(p_all, sh[:, COL_R:COL_K], sh[:, COL_K:COL_V], sh[:, COL_V:COL_Q], sh[:, COL_LORA:],
                              prep_w, row0=n_prompt, rows=dec_batch)
        o_rwkv_s, wkv_s = _rwkv_step(state_rwkv_wkv[l], *prep_s, *head_par, bt=8)
        p_s = p_all[n_prompt:]
        o_attn_s, kwin_s, vwin_s = _swa_sample(p_s[:, COL_Q:COL_AK], p_s[:, COL_AK:COL_AV], p_s[:, COL_AV:COL_LORA],
                                               cache_swa_k[l], cache_swa_v[l], slopes, attn_sinks[l], bt=8)

        x = _outproj(x, o_rwkv_p, o_attn_p, o_rwkv_s, o_attn_s, w_out[l].astype(BF16), tm=tm)
        last_layer = l == depth - 1
        x = _ffn(x, ffn2_norm[l], ffn2_w_gate[l], ffn2_w_up[l], ffn2_w_down[l], final_norm if last_layer else None,
                 tm=tm_ffn, tf=tf, split_tail=dec_batch if last_layer else 0)

        wb = min(WINDOW, seq)
        tails = [p_all[(b + 1) * seq - wb:(b + 1) * seq] for b in range(batch)]
        last = jnp.concatenate([t[wb - 1:] for t in tails], axis=0)
        shift_cols = lambda a: jnp.concatenate([a[:, COL_R:COL_Q], a[:, COL_LORA:COL_LORA + LORA_WIDTH]], axis=-1)
        for lst, val in zip(outs, (
                shift_cols(last), wkv_p,
                jnp.stack([t[:, COL_AK:COL_AV] for t in tails]).reshape(batch, wb, N_KV_HEADS, HEAD_DIM),
                jnp.stack([t[:, COL_AV:COL_LORA] for t in tails]).reshape(batch, wb, N_KV_HEADS, HEAD_DIM),
                shift_cols(p_s), wkv_s,
                kwin_s.reshape(dec_batch, win, N_KV_HEADS, HEAD_DIM),
                vwin_s.reshape(dec_batch, win, N_KV_HEADS, HEAD_DIM))):
            lst.append(val)

    y_prompt, y_sample = x
    return ((y_prompt.reshape(batch, seq, d_model), y_sample.reshape(dec_batch, 1, d_model))
            + tuple(jnp.stack(o) for o in outs))
```
